```python
import math
import jax, jax.numpy as jnp
from jax import lax
import numpy as np

D_MODEL = 2048
BATCH = 1
SEQ = 16384
DEPTH = 1
DEC_BATCH = 4
DEC_SEQ = 2048
PAST_LEN = 128

RWKV_HEADS = 16
RWKV_HEAD_DIM = 64
RWKV_WIDTH = RWKV_HEADS * RWKV_HEAD_DIM
DECAY_LORA = 96
AAA_LORA = 96
GATE_LORA = 256
GN_EPS = 64e-5
RWKV_IN = 3 * RWKV_WIDTH + 2 * DECAY_LORA + 2 * AAA_LORA + GATE_LORA

MLA_HEADS = 8
Q_LORA = 512
KV_LORA = 512
QK_NOPE = 128
QK_ROPE = 64
QK_DIM = QK_NOPE + QK_ROPE
V_HEAD = 128
MLA_WIDTH = MLA_HEADS * V_HEAD
MLA_IN = Q_LORA + KV_LORA + QK_ROPE
ROPE_BASE = 10000.0
Q_BLOCK = 128

GATE_IN = 2 * D_MODEL
IN_COLS = RWKV_IN + MLA_IN + GATE_IN
D_FF = 5632
NORM_EPS = 1e-6

kernel_name = 'hybrid_rwkv7_mla_macaron_encoder'

F32 = jnp.float32


def rms_norm(x, g, eps=NORM_EPS):
    xf = x.astype(F32)
    y = xf * lax.rsqrt(jnp.mean(xf * xf, axis=-1, keepdims=True) + eps)
    return (y * g.astype(F32)).astype(x.dtype)


def swiglu_ffn(h, w_in, w_out):
    gate, up = jnp.split(h @ w_in, 2, axis=-1)
    return (jax.nn.silu(gate) * up) @ w_out


def split_cols(u, sizes):
    idx = np.cumsum(np.array(sizes))[:-1].tolist()
    return jnp.split(u, idx, axis=-1)


def centred_token_shift(u, mu):
    zero = jnp.zeros_like(u[:, :1])
    prev = jnp.concatenate([zero, u[:, :-1]], axis=1)
    nxt = jnp.concatenate([u[:, 1:], zero], axis=1)
    return u + mu * (0.5 * (prev + nxt) - u)


def head_group_norm(x, g, b):
    mean = jnp.mean(x, axis=-1, keepdims=True)
    xc = x - mean
    var = jnp.mean(xc * xc, axis=-1, keepdims=True)
    y = xc * lax.rsqrt(var + GN_EPS)
    bsz, t = x.shape[:2]
    return y.reshape(bsz, t, -1) * g.astype(F32) + b.astype(F32)


def wkv7_scan(r, decay, k, v, a, b, reverse):
    bsz, _, h, n = r.shape

    def step(S, inp):
        r_t, w_t, k_t, v_t, a_t, b_t = inp
        sa = jnp.einsum('bhij,bhj->bhi', S, a_t)
        S = S * w_t[:, :, None, :] + sa[..., None] * b_t[:, :, None, :] + v_t[..., None] * k_t[:, :, None, :]
        o = jnp.einsum('bhij,bhj->bhi', S, r_t)
        return S, o

    xs = tuple(jnp.swapaxes(t, 0, 1) for t in (r, decay, k, v, a, b))
    S0 = jnp.zeros((bsz, h, n, n), F32)
    _, out = lax.scan(step, S0, xs, reverse=reverse)
    return jnp.swapaxes(out, 0, 1)


def rwkv7_bidirectional(z, mu, w0, w2, a0, a2, g2, k_k, k_a, r_k, ln_g, ln_b):
    bsz, t = z.shape[:2]
    z = centred_token_shift(z, mu)
    r, k, v, wd_f, wd_b, ad_f, ad_b, gd = split_cols(
        z, (RWKV_WIDTH, RWKV_WIDTH, RWKV_WIDTH, DECAY_LORA, DECAY_LORA, AAA_LORA, AAA_LORA, GATE_LORA))
    shape4 = (bsz, t, RWKV_HEADS, RWKV_HEAD_DIM)
    g = (jax.nn.sigmoid(gd) @ g2).astype(F32)
    r4 = r.astype(F32).reshape(shape4)
    v4 = v.astype(F32).reshape(shape4)
    kk = (k * k_k).astype(F32).reshape(shape4)
    kk = kk * lax.rsqrt(jnp.sum(kk * kk, axis=-1, keepdims=True) + 1e-12)
    out = None
    for d, (wd, ad) in enumerate(((wd_f, ad_f), (wd_b, ad_b))):
        w_pre = (w0[d] + jnp.tanh(wd) @ w2[d]).astype(F32)
        decay = jnp.exp(-jnp.exp(-jax.nn.softplus(-w_pre) - 0.5)).reshape(shape4)
        a = jax.nn.sigmoid((a0[d] + ad @ a2[d]).astype(F32))
        k_d = (k.astype(F32) * (1.0 + (a - 1.0) * k_a.astype(F32))).reshape(shape4)
        a4 = a.reshape(shape4)
        wkv = wkv7_scan(r4, decay, k_d, v4, -kk, kk * a4, reverse=(d == 1))
        bonus = jnp.sum(r4 * k_d * r_k.astype(F32), axis=-1, keepdims=True) * v4
        term = head_group_norm(wkv, ln_g, ln_b) + bonus.reshape(bsz, t, RWKV_WIDTH)
        out = term if out is None else out + term
    return (out * g).astype(z.dtype)


def rope_tables(t):
    inv = ROPE_BASE ** (-jnp.arange(0, QK_ROPE, 2, dtype=F32) / QK_ROPE)
    ang = jnp.arange(t, dtype=F32)[:, None] * inv[None, :]
    return jnp.cos(ang), jnp.sin(ang)


def apply_rope_tail(u, cos, sin):
    nope = u[..., :QK_NOPE]
    x1, x2 = jnp.split(u[..., QK_NOPE:].astype(F32), 2, axis=-1)
    c = cos[None, :, None, :]
    s = sin[None, :, None, :]
    rot = jnp.concatenate([x1 * c - x2 * s, x2 * c + x1 * s], axis=-1).astype(u.dtype)
    return jnp.concatenate([nope, rot], axis=-1)


def mla_attention(z, q_norm, w_uq, kv_norm, w_ukv, q_gain, k_gain):
    bsz, t = z.shape[:2]
    cq, ckv, kr = split_cols(z, (Q_LORA, KV_LORA, QK_ROPE))
    q = (rms_norm(cq, q_norm) @ w_uq).reshape(bsz, t, MLA_HEADS, QK_DIM)
    kv = (rms_norm(ckv, kv_norm) @ w_ukv).reshape(bsz, t, MLA_HEADS, QK_NOPE + V_HEAD)
    k_nope, v = jnp.split(kv, [QK_NOPE], axis=-1)
    k_rope = jnp.broadcast_to(kr[:, :, None, :], (bsz, t, MLA_HEADS, QK_ROPE))
    k = jnp.concatenate([k_nope, k_rope], axis=-1)
    q = rms_norm(q, q_gain)
    k = rms_norm(k, k_gain)
    cos, sin = rope_tables(t)
    q = apply_rope_tail(q, cos, sin)
    k = apply_rope_tail(k, cos, sin)
    nb = t // Q_BLOCK
    qb = jnp.moveaxis(q.reshape(bsz, nb, Q_BLOCK, MLA_HEADS, QK_DIM), 1, 0)
    scale = 1.0 / math.sqrt(QK_DIM)

    def block(q_blk):
        s = jnp.einsum('bqhd,bkhd->bhqk', q_blk, k, preferred_element_type=F32) * scale
        p = jax.nn.softmax(s, axis=-1).astype(v.dtype)
        return jnp.einsum('bhqk,bkhd->bqhd', p, v)

    o = lax.map(block, qb)
    return jnp.moveaxis(o, 0, 1).reshape(bsz, t, MLA_WIDTH)


def encoder_layer(x, p):
    x = x + 0.5 * swiglu_ffn(rms_norm(x, p['ffn1_norm']), p['ffn1_w_in'], p['ffn1_w_out'])
    h = rms_norm(x, p['mix_norm'])
    z = h @ p['w_in']
    z_rwkv, z_mla, z_gate = jnp.split(z, [RWKV_IN, RWKV_IN + MLA_IN], axis=-1)
    o_a = rwkv7_bidirectional(z_rwkv, p['rwkv_mu'], p['rwkv_w0'], p['rwkv_w2'], p['rwkv_a0'], p['rwkv_a2'],
                              p['rwkv_g2'], p['rwkv_k_k'], p['rwkv_k_a'], p['rwkv_r_k'],
                              p['rwkv_ln_g'], p['rwkv_ln_b'])
    o_b = mla_attention(z_mla, p['mla_q_norm'], p['mla_w_uq'], p['mla_kv_norm'], p['mla_w_ukv'],
                        p['mla_q_gain'], p['mla_k_gain'])
    gate_a, gate_b = jnp.split(z_gate, 2, axis=-1)
    merged = (jax.nn.sigmoid(gate_a) * (o_a @ p['w_branch_a'])
              + jax.nn.sigmoid(gate_b) * (o_b @ p['w_branch_b']))
    x = x + merged @ p['w_out']
    x = x + 0.5 * swiglu_ffn(rms_norm(x, p['ffn2_norm']), p['ffn2_w_in'], p['ffn2_w_out'])
    return rms_norm(x, p['out_norm'])


def setup_inputs(seed: int = 0) -> dict:
    key = jax.random.key(seed)
    ks = iter(jax.random.split(key, 40))
    L = DEPTH

    def nrm(shape, scale):
        return scale * jax.random.normal(next(ks), shape, F32)

    def gain(n):
        return 1.0 + nrm((L, n), 0.02)

    d = {}
    d['x_prompt'] = nrm((BATCH, SEQ, D_MODEL), 1.0)
    d['x_sample'] = nrm((DEC_BATCH, DEC_SEQ, D_MODEL), 1.0)
    d['ffn1_norm'] = gain(D_MODEL)
    d['ffn1_w_in'] = nrm((L, D_MODEL, 2 * D_FF), D_MODEL ** -0.5)
    d['ffn1_w_out'] = nrm((L, D_FF, D_MODEL), D_FF ** -0.5)
    d['mix_norm'] = gain(D_MODEL)
    d['w_in'] = nrm((L, D_MODEL, IN_COLS), D_MODEL ** -0.5)
    d['rwkv_mu'] = jax.random.uniform(next(ks), (L, RWKV_IN), F32, 0.0, 1.0)
    d['rwkv_w0'] = jax.random.uniform(next(ks), (L, 2, RWKV_WIDTH), F32, -6.0, -1.0)
    d['rwkv_w2'] = nrm((L, 2, DECAY_LORA, RWKV_WIDTH), DECAY_LORA ** -0.5)
    d['rwkv_a0'] = nrm((L, 2, RWKV_WIDTH), 0.1)
    d['rwkv_a2'] = nrm((L, 2, AAA_LORA, RWKV_WIDTH), AAA_LORA ** -0.5)
    d['rwkv_g2'] = nrm((L, GATE_LORA, RWKV_WIDTH), GATE_LORA ** -0.5)
    d['rwkv_k_k'] = 0.85 + nrm((L, RWKV_WIDTH), 0.05)
    d['rwkv_k_a'] = 1.0 + nrm((L, RWKV_WIDTH), 0.05)
    d['rwkv_r_k'] = nrm((L, RWKV_HEADS, RWKV_HEAD_DIM), 0.1)
    d['rwkv_ln_g'] = gain(RWKV_WIDTH)
    d['rwkv_ln_b'] = nrm((L, RWKV_WIDTH), 0.02)
    d['mla_q_norm'] = gain(Q_LORA)
    d['mla_w_uq'] = nrm((L, Q_LORA, MLA_HEADS * QK_DIM), Q_LORA ** -0.5)
    d['mla_kv_norm'] = gain(KV_LORA)
    d['mla_w_ukv'] = nrm((L, KV_LORA, MLA_HEADS * (QK_NOPE + V_HEAD)), KV_LORA ** -0.5)
    d['mla_q_gain'] = gain(QK_DIM)
    d['mla_k_gain'] = gain(QK_DIM)
    d['w_branch_a'] = nrm((L, RWKV_WIDTH, D_MODEL), RWKV_WIDTH ** -0.5)
    d['w_branch_b'] = nrm((L, MLA_WIDTH, D_MODEL), MLA_WIDTH ** -0.5)
    d['w_out'] = nrm((L, D_MODEL, D_MODEL), D_MODEL ** -0.5)
    d['ffn2_norm'] = gain(D_MODEL)
    d['ffn2_w_in'] = nrm((L, D_MODEL, 2 * D_FF), D_MODEL ** -0.5)
    d['ffn2_w_out'] = nrm((L, D_FF, D_MODEL), D_FF ** -0.5)
    d['out_norm'] = gain(D_MODEL)
    return d


def reference(x_prompt, x_sample, ffn1_norm, ffn1_w_in, ffn1_w_out, mix_norm, w_in, rwkv_mu, rwkv_w0,
              rwkv_w2, rwkv_a0, rwkv_a2, rwkv_g2, rwkv_k_k, rwkv_k_a, rwkv_r_k, rwkv_ln_g, rwkv_ln_b,
              mla_q_norm, mla_w_uq, mla_kv_norm, mla_w_ukv, mla_q_gain, mla_k_gain, w_branch_a,
              w_branch_b, w_out, ffn2_norm, ffn2_w_in, ffn2_w_out, out_norm):
    def layer_params(l):
        return dict(
            ffn1_norm=ffn1_norm[l], ffn1_w_in=ffn1_w_in[l], ffn1_w_out=ffn1_w_out[l],
            mix_norm=mix_norm[l], w_in=w_in[l], rwkv_mu=rwkv_mu[l], rwkv_w0=rwkv_w0[l],
            rwkv_w2=rwkv_w2[l], rwkv_a0=rwkv_a0[l], rwkv_a2=rwkv_a2[l], rwkv_g2=rwkv_g2[l],
            rwkv_k_k=rwkv_k_k[l], rwkv_k_a=rwkv_k_a[l], rwkv_r_k=rwkv_r_k[l],
            rwkv_ln_g=rwkv_ln_g[l], rwkv_ln_b=rwkv_ln_b[l],
            mla_q_norm=mla_q_norm[l], mla_w_uq=mla_w_uq[l], mla_kv_norm=mla_kv_norm[l],
            mla_w_ukv=mla_w_ukv[l], mla_q_gain=mla_q_gain[l], mla_k_gain=mla_k_gain[l],
            w_branch_a=w_branch_a[l], w_branch_b=w_branch_b[l], w_out=w_out[l],
            ffn2_norm=ffn2_norm[l], ffn2_w_in=ffn2_w_in[l], ffn2_w_out=ffn2_w_out[l],
            out_norm=out_norm[l])

    y_prompt = x_prompt
    y_sample = x_sample
    for l in range(DEPTH):
        p = layer_params(l)
        y_prompt = encoder_layer(y_prompt, p)
        y_sample = encoder_layer(y_sample, p)
    return (y_prompt, y_sample)
```

```python
import functools
import math

import jax
import jax.numpy as jnp
from jax import lax
from jax.experimental import pallas as pl
from jax.experimental.pallas import tpu as pltpu

F32 = jnp.float32
BF16 = jnp.bfloat16

D_MODEL = 2048
D_FF = 5632
NORM_EPS = 1e-6

RWKV_HEADS = 16
RWKV_HEAD_DIM = 64
RWKV_WIDTH = RWKV_HEADS * RWKV_HEAD_DIM
DECAY_LORA = 96
AAA_LORA = 96
GATE_LORA = 256
LORA_SLAB = 2 * DECAY_LORA + 2 * AAA_LORA
GN_EPS = 64e-5
RWKV_IN = 3 * RWKV_WIDTH + LORA_SLAB + GATE_LORA

MLA_HEADS = 8
Q_LORA = 512
KV_LORA = 512
QK_NOPE = 128
QK_ROPE = 64
QK_DIM = QK_NOPE + QK_ROPE
V_HEAD = 128
MLA_WIDTH = MLA_HEADS * V_HEAD
MLA_IN = Q_LORA + KV_LORA + QK_ROPE
MLA_COLS = MLA_IN + QK_ROPE
ROPE_BASE = 10000.0
GATE_IN = 2 * D_MODEL

LANES = 128
SUBLANES = 8
VMEM_LIMIT_BYTES = 56 * 1024 * 1024

WKV_CHUNK = 64
WKV_DIAG = 16


def _params(*semantics):
    return pltpu.CompilerParams(dimension_semantics=semantics, vmem_limit_bytes=VMEM_LIMIT_BYTES)


def _const_spec(shape):
    zeros = (0,) * len(shape)
    return pl.BlockSpec(shape, lambda *_: zeros, pipeline_mode=pl.Buffered(1))


def _rms(x, g):
    ms = jnp.mean(x * x, axis=-1, keepdims=True)
    return x * lax.rsqrt(ms + NORM_EPS) * g


def _sigmoid(x):
    return 1.0 / (1.0 + jnp.exp(-x))


def _bf16_parts(x, n):
    if x.dtype == BF16:
        return [x]
    parts = []
    rem = x
    for i in range(n):
        p = rem.astype(BF16)
        parts.append(p)
        if i + 1 < n:
            rem = rem - p.astype(F32)
    return parts


_NN = (((1,), (0,)), ((), ()))
_NT = (((1,), (1,)), ((), ()))
_TN = (((0,), (0,)), ((), ()))


def _dotp(a, b, dims=_NN, pa=1, pb=1):
    a_parts = _bf16_parts(a, pa)
    b_parts = _bf16_parts(b, pb)
    order = max(len(a_parts), len(b_parts)) - 1
    acc = None
    for i, ap in enumerate(a_parts):
        for j, bp in enumerate(b_parts):
            if i + j > order:
                continue
            t = lax.dot_general(ap, bp, dims, preferred_element_type=F32)
            acc = t if acc is None else acc + t
    return acc


def _ffn_kernel(x_ref, g_ref, wg_ref, wu_ref, wo_ref, *rest, final_norm):
    if final_norm:
        gf_ref, o_ref, xn_ref, acc_ref = rest
    else:
        o_ref, xn_ref, acc_ref = rest
    j = pl.program_id(1)

    @pl.when(j == 0)
    def _():
        xn_ref[...] = _rms(x_ref[...], g_ref[...]).astype(BF16)
        acc_ref[...] = jnp.zeros_like(acc_ref)

    xn = xn_ref[...]
    gate = jnp.dot(xn, wg_ref[...], preferred_element_type=F32)
    up = jnp.dot(xn, wu_ref[...], preferred_element_type=F32)
    h = (gate * _sigmoid(gate) * up).astype(BF16)
    acc_ref[...] += jnp.dot(h, wo_ref[...], preferred_element_type=F32)

    @pl.when(j == pl.num_programs(1) - 1)
    def _():
        y = x_ref[...] + 0.5 * acc_ref[...]
        if final_norm:
            y = _rms(y, gf_ref[...])
        o_ref[...] = y


def _ffn(x, g, w_in, w_out, g_final=None, *, tm=512, tf=512):
    m, d = x.shape
    n_ff = w_out.shape[0]
    nj = n_ff // tf
    in_specs = [
        pl.BlockSpec((tm, d), lambda i, j: (i, 0)),
        pl.BlockSpec((1, d), lambda i, j: (0, 0)),
        pl.BlockSpec((d, tf), lambda i, j: (0, j)),
        pl.BlockSpec((d, tf), lambda i, j: (0, j + nj)),
        pl.BlockSpec((tf, d), lambda i, j: (j, 0)),
    ]
    args = [x, g, w_in, w_in, w_out]
    if g_final is not None:
        in_specs.append(pl.BlockSpec((1, d), lambda i, j: (0, 0)))
        args.append(g_final)
    return pl.pallas_call(
        functools.partial(_ffn_kernel, final_norm=g_final is not None),
        out_shape=jax.ShapeDtypeStruct((m, d), F32),
        grid=(m // tm, nj),
        in_specs=in_specs,
        out_specs=pl.BlockSpec((tm, d), lambda i, j: (i, 0)),
        scratch_shapes=[pltpu.VMEM((tm, d), BF16), pltpu.VMEM((tm, d), F32)],
        compiler_params=_params("parallel", "arbitrary"),
        name="ffn",
    )(*args)


def _norm_matmul_kernel(x_ref, g_ref, w_ref, o_ref, xn_ref):
    @pl.when(pl.program_id(1) == 0)
    def _():
        xn_ref[...] = _rms(x_ref[...], g_ref[...]).astype(BF16)

    o_ref[...] = jnp.dot(xn_ref[...], w_ref[...], preferred_element_type=F32)


def _norm_matmul(x, g, w, *, tm, tn, name):
    m, d = x.shape
    n = w.shape[1]
    w_spec = _const_spec((d, n)) if tn == n else pl.BlockSpec((d, tn), lambda i, j: (0, j))
    return pl.pallas_call(
        _norm_matmul_kernel,
        out_shape=jax.ShapeDtypeStruct((m, n), F32),
        grid=(m // tm, n // tn),
        in_specs=[
            pl.BlockSpec((tm, d), lambda i, j: (i, 0)),
            pl.BlockSpec((1, d), lambda i, j: (0, 0)),
            w_spec,
        ],
        out_specs=pl.BlockSpec((tm, tn), lambda i, j: (i, j)),
        scratch_shapes=[pltpu.VMEM((tm, d), BF16)],
        compiler_params=_params("parallel", "arbitrary"),
        name=name,
    )(x, g, w)


def _head_sum(x):
    r = lax.broadcasted_iota(jnp.int32, (LANES, LANES), 0) // RWKV_HEAD_DIM
    c = lax.broadcasted_iota(jnp.int32, (LANES, LANES), 1) // RWKV_HEAD_DIM
    ones = (r == c).astype(BF16)
    tiles = [
        _dotp(x[:, j * LANES:(j + 1) * LANES], ones, pa=3, pb=1)
        for j in range(x.shape[1] // LANES)
    ]
    return jnp.concatenate(tiles, axis=1)


def _rwkv_prep_kernel(z_ref, zp_ref, zn_ref, mu_ref, w0_ref, a0_ref, w2f_ref, w2b_ref, a2f_ref,
                      a2b_ref, g2_ref, kk_ref, ka_ref, rk_ref,
                      r_out, v_out, kkn_out, g_out, bonus_out,
                      lwf_out, kdf_out, asf_out, lwb_out, kdb_out, asb_out):
    j = pl.program_id(1)
    nj = pl.num_programs(1)
    z = z_ref[0]
    tm = z.shape[0]
    row = lax.broadcasted_iota(jnp.int32, z.shape, 0)
    halo_prev = jnp.where(j > 0, zp_ref[0, SUBLANES - 1:SUBLANES, :], 0.0)
    halo_next = jnp.where(j < nj - 1, zn_ref[0, 0:1, :], 0.0)
    prev = jnp.where(row == 0, halo_prev, pltpu.roll(z, 1, axis=0))
    nxt = jnp.where(row == tm - 1, halo_next, pltpu.roll(z, tm - 1, axis=0))
    zs = z + mu_ref[...] * (0.5 * (prev + nxt) - z)

    w = RWKV_WIDTH
    r = zs[:, 0:w]
    k = zs[:, w:2 * w]
    v = zs[:, 2 * w:3 * w]
    lora = zs[:, 3 * w:3 * w + LORA_SLAB]
    gd = zs[:, 3 * w + LORA_SLAB:]

    g_out[0] = jnp.dot(_sigmoid(gd).astype(BF16), g2_ref[...], preferred_element_type=F32)
    kk = k * kk_ref[...]
    kk = kk * lax.rsqrt(_head_sum(kk * kk) + 1e-12)
    r_out[0] = r
    v_out[0] = v
    kkn_out[0] = kk

    lora_t = jnp.tanh(lora).astype(BF16)
    lora_l = lora.astype(BF16)
    bonus = None
    for d, (w2_ref, a2_ref, lw_out, kd_out, as_out) in enumerate(
            ((w2f_ref, a2f_ref, lwf_out, kdf_out, asf_out), (w2b_ref, a2b_ref, lwb_out, kdb_out, asb_out))):
        w_pre = w0_ref[d:d + 1, :] + jnp.dot(lora_t, w2_ref[...], preferred_element_type=F32)
        lw_out[0] = -math.exp(-0.5) * _sigmoid(w_pre)
        a = _sigmoid(a0_ref[d:d + 1, :] + jnp.dot(lora_l, a2_ref[...], preferred_element_type=F32))
        k_d = k * (1.0 + (a - 1.0) * ka_ref[...])
        kd_out[0] = k_d
        as_out[0] = a
        b_d = _head_sum(r * k_d * rk_ref[...]) * v
        bonus = b_d if bonus is None else bonus + b_d
    bonus_out[0] = bonus


def _rwkv_prep(z, mu, w0, a0, w2f, w2b, a2f, a2b, g2, k_k, k_a, r_k, *, tm=256):
    b, t, zc = z.shape
    w = RWKV_WIDTH
    nb = tm // SUBLANES
    last8 = t // SUBLANES - 1
    out = jax.ShapeDtypeStruct((b, t, w), F32)
    row_spec = pl.BlockSpec((1, tm, w), lambda i, j: (i, j, 0))
    return pl.pallas_call(
        _rwkv_prep_kernel,
        out_shape=[out] * 11,
        grid=(b, t // tm),
        in_specs=[
            pl.BlockSpec((1, tm, zc), lambda i, j: (i, j, 0)),
            pl.BlockSpec((1, SUBLANES, zc), lambda i, j: (i, jnp.maximum(j * nb - 1, 0), 0)),
            pl.BlockSpec((1, SUBLANES, zc), lambda i, j: (i, jnp.minimum((j + 1) * nb, last8), 0)),
            _const_spec((1, zc)),
            _const_spec((2, w)),
            _const_spec((2, w)),
            _const_spec((LORA_SLAB, w)),
            _const_spec((LORA_SLAB, w)),
            _const_spec((LORA_SLAB, w)),
            _const_spec((LORA_SLAB, w)),
            _const_spec((GATE_LORA, w)),
            _const_spec((1, w)),
            _const_spec((1, w)),
            _const_spec((1, w)),
        ],
        out_specs=[row_spec] * 11,
        compiler_params=_params("parallel", "parallel"),
        name="rwkv_prep",
    )(z, z, z, mu, w0, a0, w2f, w2b, a2f, a2b, g2, k_k, k_a, r_k)


def _wkv_chunk(r, lw, k, v, kk, asig, h_state, reverse):
    c = r.shape[0]
    n = 2 * c
    p2 = 2
    mm = functools.partial(_dotp, pa=p2, pb=p2)
    a = -kk
    b = kk * asig

    ti = lax.broadcasted_iota(jnp.int32, (c, c), 0)
    si = lax.broadcasted_iota(jnp.int32, (c, c), 1)
    tri = ((si >= ti) if reverse else (si <= ti)).astype(BF16)
    cum = _dotp(tri, lw, pa=1, pb=3)
    cum_x = cum - lw
    total = jnp.sum(lw, axis=0, keepdims=True)
    e_in = jnp.exp(cum)
    e_ex = jnp.exp(cum_x)
    e_neg = jnp.exp(-cum)
    e_rest = jnp.exp(total - cum)

    lane = lax.broadcasted_iota(jnp.int32, (c, LANES), 1)
    first = lane < RWKV_HEAD_DIM

    def stack(x):
        return jnp.concatenate([jnp.where(first, x, 0.0), jnp.where(first, 0.0, x)], axis=0)

    lhs = jnp.concatenate([stack(a * e_ex), stack(r * e_in)], axis=0)
    rhs = jnp.concatenate([stack(b * e_neg), stack(k * e_neg)], axis=0)
    aa = mm(lhs, rhs, _NT)

    row = lax.broadcasted_iota(jnp.int32, (n, n), 0)
    col = lax.broadcasted_iota(jnp.int32, (n, n), 1)
    t_in = row % c
    s_in = col % c
    strict = (s_in > t_in) if reverse else (s_in < t_in)
    incl = (s_in >= t_in) if reverse else (s_in <= t_in)
    n_ab = jnp.where(strict, aa[:n, :n], 0.0)
    a_ak = jnp.where(strict, aa[:n, n:], 0.0)
    a_rb = jnp.where(incl, aa[n:, :n], 0.0)
    a_rk = jnp.where(incl, aa[n:, n:], 0.0)

    eye = (row == col).astype(F32)
    diag = (row // WKV_DIAG) == (col // WKV_DIAG)
    n_d = jnp.where(diag, n_ab, 0.0)
    n_o = jnp.where(diag, 0.0, n_ab)
    x = eye + n_d
    pw = n_d
    for _ in range(int(math.log2(WKV_DIAG)) - 1):
        pw = mm(pw, pw)
        x = x + mm(x, pw)
    t_d = x
    e1 = mm(t_d, n_o)
    y = eye + e1
    pw = e1
    for _ in range(int(math.log2(c // WKV_DIAG)) - 1):
        pw = mm(pw, pw)
        y = y + mm(y, pw)
    t_inv = mm(y, t_d)

    vs = stack(v)
    av = mm(jnp.concatenate([a_ak, a_rk], axis=0), vs)
    khv = mm(stack(k * e_rest), vs, _TN)
    x1 = mm(lhs, h_state)
    us = mm(t_inv, x1[:n] + av[:n])
    os_ = x1[n:] + mm(a_rb, us) + av[n:]
    out = os_[:c] + os_[c:]
    e_col = jnp.exp(_dotp(lw, jnp.ones((c, LANES), BF16), _TN, pa=3, pb=1))
    h_new = e_col * h_state + mm(stack(b * e_rest), us, _TN) + khv
    return out, h_new


def _wkv_kernel(rf_ref, vf_ref, kkf_ref, lwf_ref, kdf_ref, asf_ref,
                rb_ref, vb_ref, kkb_ref, lwb_ref, kdb_ref, asb_ref,
                of_ref, ob_ref, hf_ref, hb_ref):
    @pl.when(pl.program_id(2) == 0)
    def _():
        hf_ref[...] = jnp.zeros_like(hf_ref)
        hb_ref[...] = jnp.zeros_like(hb_ref)

    o_f, h_f = _wkv_chunk(rf_ref[0], lwf_ref[0], kdf_ref[0], vf_ref[0], kkf_ref[0], asf_ref[0],
                          hf_ref[...], reverse=False)
    of_ref[0] = o_f
    hf_ref[...] = h_f
    o_b, h_b = _wkv_chunk(rb_ref[0], lwb_ref[0], kdb_ref[0], vb_ref[0], kkb_ref[0], asb_ref[0],
                          hb_ref[...], reverse=True)
    ob_ref[0] = o_b
    hb_ref[...] = h_b


def _wkv_scan(r, v, kk, lw_f, kd_f, as_f, lw_b, kd_b, as_b):
    b, t, w = r.shape
    c = WKV_CHUNK
    nc = t // c
    fwd = pl.BlockSpec((1, c, LANES), lambda i, p, s: (i, s, p))
    bwd = pl.BlockSpec((1, c, LANES), lambda i, p, s: (i, nc - 1 - s, p))
    out = jax.ShapeDtypeStruct((b, t, w), F32)
    return pl.pallas_call(
        _wkv_kernel,
        out_shape=[out, out],
        grid=(b, w // LANES, nc),
        in_specs=[fwd] * 6 + [bwd] * 6,
        out_specs=[fwd, bwd],
        scratch_shapes=[pltpu.VMEM((LANES, LANES), F32), pltpu.VMEM((LANES, LANES), F32)],
        compiler_params=_params("parallel", "parallel", "arbitrary"),
        name="wkv_scan",
    )(r, v, kk, lw_f, kd_f, as_f, r, v, kk, lw_b, kd_b, as_b)


def _mla_prep_kernel(x_ref, g_ref, wz_ref, qn_ref, kvn_ref, wqn_ref, wqr_ref, wqs_ref, wkn_ref, wv_ref,
                     qg_ref, qgs_ref, kg_ref, kgs_ref, cos_ref, sin_ref, q_out, k_out, v_out):
    xn = _rms(x_ref[0], g_ref[...]).astype(BF16)
    z = jnp.dot(xn, wz_ref[...], preferred_element_type=F32)
    cq = _rms(z[:, :Q_LORA], qn_ref[...]).astype(BF16)
    ckv = _rms(z[:, Q_LORA:Q_LORA + KV_LORA], kvn_ref[...]).astype(BF16)
    kr = z[:, MLA_IN - QK_ROPE:MLA_IN]
    krs = z[:, MLA_IN:MLA_COLS]
    q_nope = jnp.dot(cq, wqn_ref[...], preferred_element_type=F32)
    q_rope = jnp.dot(cq, wqr_ref[...], preferred_element_type=F32)
    q_swap = jnp.dot(cq, wqs_ref[...], preferred_element_type=F32)
    k_nope = jnp.dot(ckv, wkn_ref[...], preferred_element_type=F32)
    val = jnp.dot(ckv, wv_ref[...], preferred_element_type=F32)
    cos = cos_ref[...]
    sin = sin_ref[...]
    scale = 1.0 / math.sqrt(QK_DIM)
    kr_ss = jnp.sum(kr * kr, axis=-1, keepdims=True)
    for h in range(MLA_HEADS):
        nope = slice(h * QK_NOPE, (h + 1) * QK_NOPE)
        rope = slice(h * QK_ROPE, (h + 1) * QK_ROPE)
        qn, qr, qs = q_nope[:, nope], q_rope[:, rope], q_swap[:, rope]
        ss = jnp.sum(qn * qn, axis=-1, keepdims=True) + jnp.sum(qr * qr, axis=-1, keepdims=True)
        rs = lax.rsqrt(ss / QK_DIM + NORM_EPS)
        q_out[0, h, :, :QK_NOPE] = (qn * rs * qg_ref[:, :QK_NOPE] * scale).astype(BF16)
        rot = qr * rs * qg_ref[:, QK_NOPE:] * cos + qs * rs * qgs_ref[...] * sin
        q_out[0, h, :, QK_NOPE:] = (rot * scale).astype(BF16)
        kn = k_nope[:, nope]
        ss = jnp.sum(kn * kn, axis=-1, keepdims=True) + kr_ss
        rs = lax.rsqrt(ss / QK_DIM + NORM_EPS)
        k_out[0, h, :, :QK_NOPE] = (kn * rs * kg_ref[:, :QK_NOPE]).astype(BF16)
        rot = kr * rs * kg_ref[:, QK_NOPE:] * cos + krs * rs * kgs_ref[...] * sin
        k_out[0, h, :, QK_NOPE:] = rot.astype(BF16)
        v_out[0, h] = val[:, nope].astype(BF16)


def _mla_prep(x, g, wz, q_norm, kv_norm, wqn, wqr, wqs, wkn, wv, qg, qgs, kg, kgs, cos2, sin2, *, tm=256):
    b, t, d = x.shape
    h = MLA_HEADS
    return pl.pallas_call(
        _mla_prep_kernel,
        out_shape=[
            jax.ShapeDtypeStruct((b, h, t, QK_DIM), BF16),
            jax.ShapeDtypeStruct((b, h, t, QK_DIM), BF16),
            jax.ShapeDtypeStruct((b, h, t, V_HEAD), BF16),
        ],
        grid=(b, t // tm),
        in_specs=[
            pl.BlockSpec((1, tm, d), lambda i, j: (i, j, 0)),
            _const_spec((1, d)),
            _const_spec(wz.shape),
            _const_spec((1, Q_LORA)),
            _const_spec((1, KV_LORA)),
            _const_spec(wqn.shape),
            _const_spec(wqr.shape),
            _const_spec(wqs.shape),
            _const_spec(wkn.shape),
            _const_spec(wv.shape),
            _const_spec((1, QK_DIM)),
            _const_spec((1, QK_ROPE)),
            _const_spec((1, QK_DIM)),
            _const_spec((1, QK_ROPE)),
            pl.BlockSpec((tm, QK_ROPE), lambda i, j: (j, 0)),
            pl.BlockSpec((tm, QK_ROPE), lambda i, j: (j, 0)),
        ],
        out_specs=[
            pl.BlockSpec((1, h, tm, QK_DIM), lambda i, j: (i, 0, j, 0)),
            pl.BlockSpec((1, h, tm, QK_DIM), lambda i, j: (i, 0, j, 0)),
            pl.BlockSpec((1, h, tm, V_HEAD), lambda i, j: (i, 0, j, 0)),
        ],
        compiler_params=_params("parallel", "parallel"),
        name="mla_prep",
    )(x, g, wz, q_norm, kv_norm, wqn, wqr, wqs, wkn, wv, qg, qgs, kg, kgs, cos2, sin2)


def _attn_kernel(q_ref, k_ref, v_ref, o_ref, m_ref, l_ref, acc_ref):
    kv = pl.program_id(3)

    @pl.when(kv == 0)
    def _():
        m_ref[...] = jnp.full_like(m_ref, -jnp.inf)
        l_ref[...] = jnp.zeros_like(l_ref)
        acc_ref[...] = jnp.zeros_like(acc_ref)

    s = lax.dot_general(q_ref[0, 0], k_ref[0, 0], _NT, preferred_element_type=F32)
    m_old = m_ref[...]
    m_new = jnp.maximum(m_old, jnp.max(s, axis=-1, keepdims=True))
    alpha = jnp.exp(m_old - m_new)
    p = jnp.exp(s - m_new)
    l_ref[...] = alpha * l_ref[...] + jnp.sum(p, axis=-1, keepdims=True)
    acc_ref[...] = alpha * acc_ref[...] + jnp.dot(p.astype(BF16), v_ref[0, 0], preferred_element_type=F32)
    m_ref[...] = m_new

    @pl.when(kv == pl.num_programs(3) - 1)
    def _():
        o_ref[0] = (acc_ref[...] / l_ref[...]).astype(o_ref.dtype)


def _attention(q, k, v, *, tq=512, tk=512):
    b, h, t, dq = q.shape
    dv = v.shape[-1]
    tq = min(tq, t)
    tk = min(tk, t)
    return pl.pallas_call(
        _attn_kernel,
        out_shape=jax.ShapeDtypeStruct((b, t, h * dv), BF16),
        grid=(b, h, t // tq, t // tk),
        in_specs=[
            pl.BlockSpec((1, 1, tq, dq), lambda i, j, qi, ki: (i, j, qi, 0)),
            pl.BlockSpec((1, 1, tk, dq), lambda i, j, qi, ki: (i, j, ki, 0)),
            pl.BlockSpec((1, 1, tk, dv), lambda i, j, qi, ki: (i, j, ki, 0)),
        ],
        out_specs=pl.BlockSpec((1, tq, dv), lambda i, j, qi, ki: (i, qi, j)),
        scratch_shapes=[pltpu.VMEM((tq, 1), F32), pltpu.VMEM((tq, 1), F32), pltpu.VMEM((tq, dv), F32)],
        compiler_params=_params("parallel", "parallel", "parallel", "arbitrary"),
        name="mla_attention",
    )(q, k, v)


def _group_norm(x, g, b):
    mean = _head_sum(x) * (1.0 / RWKV_HEAD_DIM)
    xc = x - mean
    var = _head_sum(xc * xc) * (1.0 / RWKV_HEAD_DIM)
    return xc * lax.rsqrt(var + GN_EPS) * g + b


def _merge_kernel(x_ref, wf_ref, wb_ref, bonus_ref, g_ref, ob_ref, za_ref, zb_ref,
                  lng_ref, lnb_ref, wa_ref, wbr_ref, wo_ref, o_ref):
    lng = lng_ref[...]
    lnb = lnb_ref[...]
    o_a = (_group_norm(wf_ref[...], lng, lnb) + _group_norm(wb_ref[...], lng, lnb) + bonus_ref[...]) * g_ref[...]
    br_a = jnp.dot(o_a.astype(BF16), wa_ref[...], preferred_element_type=F32)
    br_b = jnp.dot(ob_ref[...], wbr_ref[...], preferred_element_type=F32)
    merged = _sigmoid(za_ref[...]) * br_a + _sigmoid(zb_ref[...]) * br_b
    o_ref[...] = x_ref[...] + jnp.dot(merged.astype(BF16), wo_ref[...], preferred_element_type=F32)


def _merge(x, wkv_f, wkv_b, bonus, g, o_b, z_gate, ln_g, ln_b, w_a, w_b, w_o, *, tm=256):
    m, d = x.shape
    w = RWKV_WIDTH
    row_w = pl.BlockSpec((tm, w), lambda i: (i, 0))
    return pl.pallas_call(
        _merge_kernel,
        out_shape=jax.ShapeDtypeStruct((m, d), F32),
        grid=(m // tm,),
        in_specs=[
            pl.BlockSpec((tm, d), lambda i: (i, 0)),
            row_w, row_w, row_w, row_w, row_w,
            pl.BlockSpec((tm, d), lambda i: (i, 0)),
            pl.BlockSpec((tm, d), lambda i: (i, 1)),
            _const_spec((1, w)),
            _const_spec((1, w)),
            _const_spec(w_a.shape),
            _const_spec(w_b.shape),
            _const_spec(w_o.shape),
        ],
        out_specs=pl.BlockSpec((tm, d), lambda i: (i, 0)),
        compiler_params=_params("parallel"),
        name="merge",
    )(x, wkv_f, wkv_b, bonus, g, o_b, z_gate, z_gate, ln_g, ln_b, w_a, w_b, w_o)


def _swap_halves(x, axis=-1):
    lo, hi = jnp.split(x, 2, axis=axis)
    return jnp.concatenate([hi, lo], axis=axis)


def _prepare_weights(p):
    w = {}
    w['ffn1_w_in'] = p['ffn1_w_in'].astype(BF16)
    w['ffn1_w_out'] = p['ffn1_w_out'].astype(BF16)
    w['ffn2_w_in'] = p['ffn2_w_in'].astype(BF16)
    w['ffn2_w_out'] = p['ffn2_w_out'].astype(BF16)
    w_in = p['w_in']
    w['w_rwkv'] = w_in[:, :RWKV_IN].astype(BF16)
    mla = w_in[:, RWKV_IN:RWKV_IN + MLA_IN]
    w['w_mla'] = jnp.concatenate([mla, _swap_halves(mla[:, MLA_IN - QK_ROPE:])], axis=1).astype(BF16)
    w['w_gate'] = w_in[:, RWKV_IN + MLA_IN:].astype(BF16)

    def lora_rows(mat, slot):
        z = jnp.zeros((LORA_SLAB, RWKV_WIDTH), F32)
        return lax.dynamic_update_slice(z, mat, (slot * DECAY_LORA, 0)).astype(BF16)

    w['w2f'] = lora_rows(p['rwkv_w2'][0], 0)
    w['w2b'] = lora_rows(p['rwkv_w2'][1], 1)
    w['a2f'] = lora_rows(p['rwkv_a2'][0], 2)
    w['a2b'] = lora_rows(p['rwkv_a2'][1], 3)
    w['g2'] = p['rwkv_g2'].astype(BF16)

    w_uq = p['mla_w_uq'].reshape(Q_LORA, MLA_HEADS, QK_DIM)
    w['wqn'] = w_uq[:, :, :QK_NOPE].reshape(Q_LORA, MLA_HEADS * QK_NOPE).astype(BF16)
    wqr = w_uq[:, :, QK_NOPE:]
    w['wqr'] = wqr.reshape(Q_LORA, MLA_HEADS * QK_ROPE).astype(BF16)
    w['wqs'] = _swap_halves(wqr).reshape(Q_LORA, MLA_HEADS * QK_ROPE).astype(BF16)
    w_ukv = p['mla_w_ukv'].reshape(KV_LORA, MLA_HEADS, QK_NOPE + V_HEAD)
    w['wkn'] = w_ukv[:, :, :QK_NOPE].reshape(KV_LORA, MLA_HEADS * QK_NOPE).astype(BF16)
    w['wv'] = w_ukv[:, :, QK_NOPE:].reshape(KV_LORA, MLA_HEADS * V_HEAD).astype(BF16)
    w['qgs'] = _swap_halves(p['mla_q_gain'][QK_NOPE:])[None, :]
    w['kgs'] = _swap_halves(p['mla_k_gain'][QK_NOPE:])[None, :]
    w['w_a'] = p['w_branch_a'].astype(BF16)
    w['w_b'] = p['w_branch_b'].astype(BF16)
    w['w_o'] = p['w_out'].astype(BF16)
    return w


def _rope_tables(t):
    inv = ROPE_BASE ** (-jnp.arange(0, QK_ROPE, 2, dtype=F32) / QK_ROPE)
    ang = jnp.arange(t, dtype=F32)[:, None] * inv[None, :]
    cos, sin = jnp.cos(ang), jnp.sin(ang)
    return jnp.concatenate([cos, cos], axis=1), jnp.concatenate([-sin, sin], axis=1)


def _row(vec):
    return vec.reshape(1, -1)


def _encoder_layer(x, p, w):
    b, t, d = x.shape
    m = b * t
    x0 = x.reshape(m, d)
    x1 = _ffn(x0, _row(p['ffn1_norm']), w['ffn1_w_in'], w['ffn1_w_out'])
    mix_g = _row(p['mix_norm'])
    z_rwkv = _norm_matmul(x1, mix_g, w['w_rwkv'], tm=256, tn=RWKV_IN, name="in_proj_rwkv")
    z_gate = _norm_matmul(x1, mix_g, w['w_gate'], tm=512, tn=1024, name="in_proj_gate")

    (r, v, kk, g, bonus, lw_f, kd_f, as_f, lw_b, kd_b, as_b) = _rwkv_prep(
        z_rwkv.reshape(b, t, RWKV_IN), _row(p['rwkv_mu']), p['rwkv_w0'], p['rwkv_a0'],
        w['w2f'], w['w2b'], w['a2f'], w['a2b'], w['g2'],
        _row(p['rwkv_k_k']), _row(p['rwkv_k_a']), _row(p['rwkv_r_k']))
    wkv_f, wkv_b = _wkv_scan(r, v, kk, lw_f, kd_f, as_f, lw_b, kd_b, as_b)

    cos2, sin2 = _rope_tables(t)
    q, k, val = _mla_prep(
        x1.reshape(b, t, d), mix_g, w['w_mla'], _row(p['mla_q_norm']), _row(p['mla_kv_norm']),
        w['wqn'], w['wqr'], w['wqs'], w['wkn'], w['wv'],
        _row(p['mla_q_gain']), w['qgs'], _row(p['mla_k_gain']), w['kgs'], cos2, sin2)
    o_b = _attention(q, k, val)

    flat = lambda u: u.reshape(m, -1)
    x2 = _merge(x1, flat(wkv_f), flat(wkv_b), flat(bonus), flat(g), flat(o_b), z_gate,
                _row(p['rwkv_ln_g']), _row(p['rwkv_ln_b']), w['w_a'], w['w_b'], w['w_o'])
    y = _ffn(x2, _row(p['ffn2_norm']), w['ffn2_w_in'], w['ffn2_w_out'], _row(p['out_norm']))
    return y.reshape(b, t, d)


def kernel(x_prompt, x_sample, ffn1_norm, ffn1_w_in, ffn1_w_out, mix_norm, w_in, rwkv_mu, rwkv_w0, rwkv_w2, rwkv_a0, rwkv_a2, rwkv_g2, rwkv_k_k, rwkv_k_a, rwkv_r_k, rwkv_ln_g, rwkv_ln_b, mla_q_norm, mla_w_uq, mla_kv_norm, mla_w_ukv, mla_q_gain, mla_k_gain, w_branch_a, w_branch_b, w_out, ffn2_norm, ffn2_w_in, ffn2_w_out, out_norm):
    stacked = dict(
        ffn1_norm=ffn1_norm, ffn1_w_in=ffn1_w_in, ffn1_w_out=ffn1_w_out, mix_norm=mix_norm, w_in=w_in,
        rwkv_mu=rwkv_mu, rwkv_w0=rwkv_w0, rwkv_w2=rwkv_w2, rwkv_a0=rwkv_a0, rwkv_a2=rwkv_a2,
        rwkv_g2=rwkv_g2, rwkv_k_k=rwkv_k_k, rwkv_k_a=rwkv_k_a, rwkv_r_k=rwkv_r_k,
        rwkv_ln_g=rwkv_ln_g, rwkv_ln_b=rwkv_ln_b, mla_q_norm=mla_q_norm, mla_w_uq=mla_w_uq,
        mla_kv_norm=mla_kv_norm, mla_w_ukv=mla_w_ukv, mla_q_gain=mla_q_gain, mla_k_gain=mla_k_gain,
        w_branch_a=w_branch_a, w_branch_b=w_branch_b, w_out=w_out, ffn2_norm=ffn2_norm,
        ffn2_w_in=ffn2_w_in, ffn2_w_out=ffn2_w_out, out_norm=out_norm)
    y_prompt, y_sample = x_prompt, x_sample
    for layer in range(ffn1_norm.shape[0]):
        p = {name: arr[layer] for name, arr in stacked.items()}
        p['rwkv_r_k'] = p['rwkv_r_k'].reshape(-1)
        w = _prepare_weights(p)
        y_prompt = _encoder_layer(y_prompt, p, w)
        y_sample = _encoder_layer(y_sample, p, w)
    return (y_prompt, y_sample)
```

```python
import functools
import math

import jax
import jax.numpy as jnp
from jax import lax
from jax.experimental import pallas as pl
from jax.experimental.pallas import tpu as pltpu

F32 = jnp.float32
BF16 = jnp.bfloat16

D_MODEL = 2048
D_FF = 5632
NORM_EPS = 1e-6

RWKV_HEADS = 16
RWKV_HEAD_DIM = 64
RWKV_WIDTH = RWKV_HEADS * RWKV_HEAD_DIM
DECAY_LORA = 96
AAA_LORA = 96
GATE_LORA = 256
LORA_SLAB = 2 * DECAY_LORA + 2 * AAA_LORA
GN_EPS = 64e-5
RWKV_IN = 3 * RWKV_WIDTH + LORA_SLAB + GATE_LORA

MLA_HEADS = 8
Q_LORA = 512
KV_LORA = 512
QK_NOPE = 128
QK_ROPE = 64
QK_DIM = QK_NOPE + QK_ROPE
V_HEAD = 128
MLA_WIDTH = MLA_HEADS * V_HEAD
MLA_IN = Q_LORA + KV_LORA + QK_ROPE
MLA_COLS = MLA_IN + QK_ROPE
ROPE_BASE = 10000.0
GATE_IN = 2 * D_MODEL

LANES = 128
SUBLANES = 8
VMEM_LIMIT_BYTES = 56 * 1024 * 1024

WKV_CHUNK = 64
WKV_DIAG = 16
WKV_PARTS = 1
WKV_PAIRS_PER_STEP = 4


def _params(*semantics):
    return pltpu.CompilerParams(dimension_semantics=semantics, vmem_limit_bytes=VMEM_LIMIT_BYTES)


def _const_spec(shape):
    zeros = (0,) * len(shape)
    return pl.BlockSpec(shape, lambda *_: zeros, pipeline_mode=pl.Buffered(1))


def _rms(x, g):
    ms = jnp.mean(x * x, axis=-1, keepdims=True)
    return x * lax.rsqrt(ms + NORM_EPS) * g


def _sigmoid(x):
    return 1.0 / (1.0 + jnp.exp(-x))


def _bf16_parts(x, n):
    if x.dtype == BF16:
        return [x]
    parts = []
    rem = x
    for i in range(n):
        p = rem.astype(BF16)
        parts.append(p)
        if i + 1 < n:
            rem = rem - p.astype(F32)
    return parts


_NN = (((1,), (0,)), ((), ()))
_NT = (((1,), (1,)), ((), ()))
_TN = (((0,), (0,)), ((), ()))


def _dotp(a, b, dims=_NN, pa=1, pb=1):
    a_parts = _bf16_parts(a, pa)
    b_parts = _bf16_parts(b, pb)
    order = max(len(a_parts), len(b_parts)) - 1
    acc = None
    for i, ap in enumerate(a_parts):
        for j, bp in enumerate(b_parts):
            if i + j > order:
                continue
            t = lax.dot_general(ap, bp, dims, preferred_element_type=F32)
            acc = t if acc is None else acc + t
    return acc


def _ffn_kernel(x_ref, g_ref, wg_ref, wu_ref, wo_ref, *rest, final_norm):
    if final_norm:
        gf_ref, o_ref, xn_ref, acc_ref = rest
    else:
        o_ref, xn_ref, acc_ref = rest
    j = pl.program_id(1)

    @pl.when(j == 0)
    def _():
        xn_ref[...] = _rms(x_ref[...], g_ref[...]).astype(BF16)
        acc_ref[...] = jnp.zeros_like(acc_ref)

    xn = xn_ref[...]
    gate = jnp.dot(xn, wg_ref[...], preferred_element_type=F32)
    up = jnp.dot(xn, wu_ref[...], preferred_element_type=F32)
    h = (gate * _sigmoid(gate) * up).astype(BF16)
    acc_ref[...] += jnp.dot(h, wo_ref[...], preferred_element_type=F32)

    @pl.when(j == pl.num_programs(1) - 1)
    def _():
        y = x_ref[...] + 0.5 * acc_ref[...]
        if final_norm:
            y = _rms(y, gf_ref[...])
        o_ref[...] = y


def _ffn(x, g, w_in, w_out, g_final=None, *, tm=512, tf=512):
    m, d = x.shape
    n_ff = w_out.shape[0]
    nj = n_ff // tf
    in_specs = [
        pl.BlockSpec((tm, d), lambda i, j: (i, 0)),
        pl.BlockSpec((1, d), lambda i, j: (0, 0)),
        pl.BlockSpec((d, tf), lambda i, j: (0, j)),
        pl.BlockSpec((d, tf), lambda i, j: (0, j + nj)),
        pl.BlockSpec((tf, d), lambda i, j: (j, 0)),
    ]
    args = [x, g, w_in, w_in, w_out]
    if g_final is not None:
        in_specs.append(pl.BlockSpec((1, d), lambda i, j: (0, 0)))
        args.append(g_final)
    return pl.pallas_call(
        functools.partial(_ffn_kernel, final_norm=g_final is not None),
        out_shape=jax.ShapeDtypeStruct((m, d), F32),
        grid=(m // tm, nj),
        in_specs=in_specs,
        out_specs=pl.BlockSpec((tm, d), lambda i, j: (i, 0)),
        scratch_shapes=[pltpu.VMEM((tm, d), BF16), pltpu.VMEM((tm, d), F32)],
        compiler_params=_params("parallel", "arbitrary"),
        name="ffn",
    )(*args)


def _norm_matmul_kernel(x_ref, g_ref, w_ref, o_ref, xn_ref):
    @pl.when(pl.program_id(1) == 0)
    def _():
        xn_ref[...] = _rms(x_ref[...], g_ref[...]).astype(BF16)

    o_ref[...] = jnp.dot(xn_ref[...], w_ref[...], preferred_element_type=F32)


def _norm_matmul(x, g, w, *, tm, tn, name):
    m, d = x.shape
    n = w.shape[1]
    w_spec = _const_spec((d, n)) if tn == n else pl.BlockSpec((d, tn), lambda i, j: (0, j))
    return pl.pallas_call(
        _norm_matmul_kernel,
        out_shape=jax.ShapeDtypeStruct((m, n), F32),
        grid=(m // tm, n // tn),
        in_specs=[
            pl.BlockSpec((tm, d), lambda i, j: (i, 0)),
            pl.BlockSpec((1, d), lambda i, j: (0, 0)),
            w_spec,
        ],
        out_specs=pl.BlockSpec((tm, tn), lambda i, j: (i, j)),
        scratch_shapes=[pltpu.VMEM((tm, d), BF16)],
        compiler_params=_params("parallel", "arbitrary"),
        name=name,
    )(x, g, w)


def _head_sum(x):
    r = lax.broadcasted_iota(jnp.int32, (LANES, LANES), 0) // RWKV_HEAD_DIM
    c = lax.broadcasted_iota(jnp.int32, (LANES, LANES), 1) // RWKV_HEAD_DIM
    ones = (r == c).astype(BF16)
    tiles = [
        _dotp(x[:, j * LANES:(j + 1) * LANES], ones, pa=3, pb=1)
        for j in range(x.shape[1] // LANES)
    ]
    return jnp.concatenate(tiles, axis=1)


def _rwkv_prep_kernel(z_ref, zp_ref, zn_ref, mu_ref, w0_ref, a0_ref, w2f_ref, w2b_ref, a2f_ref,
                      a2b_ref, g2_ref, kk_ref, ka_ref, rk_ref,
                      r_out, v_out, kkn_out, g_out, bonus_out,
                      lwf_out, kdf_out, asf_out, lwb_out, kdb_out, asb_out):
    j = pl.program_id(1)
    nj = pl.num_programs(1)
    z = z_ref[0]
    tm = z.shape[0]
    row = lax.broadcasted_iota(jnp.int32, z.shape, 0)
    halo_prev = jnp.where(j > 0, zp_ref[0, SUBLANES - 1:SUBLANES, :], 0.0)
    halo_next = jnp.where(j < nj - 1, zn_ref[0, 0:1, :], 0.0)
    prev = jnp.where(row == 0, halo_prev, pltpu.roll(z, 1, axis=0))
    nxt = jnp.where(row == tm - 1, halo_next, pltpu.roll(z, tm - 1, axis=0))
    zs = z + mu_ref[...] * (0.5 * (prev + nxt) - z)

    w = RWKV_WIDTH
    r = zs[:, 0:w]
    k = zs[:, w:2 * w]
    v = zs[:, 2 * w:3 * w]
    lora = zs[:, 3 * w:3 * w + LORA_SLAB]
    gd = zs[:, 3 * w + LORA_SLAB:]

    g_out[0] = jnp.dot(_sigmoid(gd).astype(BF16), g2_ref[...], preferred_element_type=F32)
    kk = k * kk_ref[...]
    kk = kk * lax.rsqrt(_head_sum(kk * kk) + 1e-12)
    r_out[0] = r
    v_out[0] = v
    kkn_out[0] = kk

    lora_t = jnp.tanh(lora).astype(BF16)
    lora_l = lora.astype(BF16)
    bonus = None
    for d, (w2_ref, a2_ref, lw_out, kd_out, as_out) in enumerate(
            ((w2f_ref, a2f_ref, lwf_out, kdf_out, asf_out), (w2b_ref, a2b_ref, lwb_out, kdb_out, asb_out))):
        w_pre = w0_ref[d:d + 1, :] + jnp.dot(lora_t, w2_ref[...], preferred_element_type=F32)
        lw_out[0] = -math.exp(-0.5) * _sigmoid(w_pre)
        a = _sigmoid(a0_ref[d:d + 1, :] + jnp.dot(lora_l, a2_ref[...], preferred_element_type=F32))
        k_d = k * (1.0 + (a - 1.0) * ka_ref[...])
        kd_out[0] = k_d
        as_out[0] = a
        b_d = _head_sum(r * k_d * rk_ref[...]) * v
        bonus = b_d if bonus is None else bonus + b_d
    bonus_out[0] = bonus


def _rwkv_prep(z, mu, w0, a0, w2f, w2b, a2f, a2b, g2, k_k, k_a, r_k, *, tm=256):
    b, t, zc = z.shape
    w = RWKV_WIDTH
    nb = tm // SUBLANES
    last8 = t // SUBLANES - 1
    out = jax.ShapeDtypeStruct((b, t, w), F32)
    row_spec = pl.BlockSpec((1, tm, w), lambda i, j: (i, j, 0))
    return pl.pallas_call(
        _rwkv_prep_kernel,
        out_shape=[out] * 11,
        grid=(b, t // tm),
        in_specs=[
            pl.BlockSpec((1, tm, zc), lambda i, j: (i, j, 0)),
            pl.BlockSpec((1, SUBLANES, zc), lambda i, j: (i, jnp.maximum(j * nb - 1, 0), 0)),
            pl.BlockSpec((1, SUBLANES, zc), lambda i, j: (i, jnp.minimum((j + 1) * nb, last8), 0)),
            _const_spec((1, zc)),
            _const_spec((2, w)),
            _const_spec((2, w)),
            _const_spec((LORA_SLAB, w)),
            _const_spec((LORA_SLAB, w)),
            _const_spec((LORA_SLAB, w)),
            _const_spec((LORA_SLAB, w)),
            _const_spec((GATE_LORA, w)),
            _const_spec((1, w)),
            _const_spec((1, w)),
            _const_spec((1, w)),
        ],
        out_specs=[row_spec] * 11,
        compiler_params=_params("parallel", "parallel"),
        name="rwkv_prep",
    )(z, z, z, mu, w0, a0, w2f, w2b, a2f, a2b, g2, k_k, k_a, r_k)


def _wkv_chunk(r, lw, k, v, kk, asig, h_state, reverse):
    c = r.shape[0]
    n = 2 * c
    mm = functools.partial(_dotp, pa=WKV_PARTS, pb=WKV_PARTS)
    a = -kk
    b = kk * asig

    ti = lax.broadcasted_iota(jnp.int32, (c, c), 0)
    si = lax.broadcasted_iota(jnp.int32, (c, c), 1)
    tri = ((si >= ti) if reverse else (si <= ti)).astype(BF16)
    cum = _dotp(tri, lw, pa=1, pb=3)
    e_col = jnp.exp(_dotp(lw, jnp.ones((c, LANES), BF16), _TN, pa=3, pb=1))
    yield

    cum_x = cum - lw
    total = jnp.sum(lw, axis=0, keepdims=True)
    e_in = jnp.exp(cum)
    e_ex = jnp.exp(cum_x)
    e_neg = jnp.exp(-cum)
    e_rest = jnp.exp(total - cum)

    lane = lax.broadcasted_iota(jnp.int32, (c, LANES), 1)
    first = lane < RWKV_HEAD_DIM

    def stack(x):
        return jnp.concatenate([jnp.where(first, x, 0.0), jnp.where(first, 0.0, x)], axis=0)

    lhs = jnp.concatenate([stack(a * e_ex), stack(r * e_in)], axis=0)
    rhs = jnp.concatenate([stack(b * e_neg), stack(k * e_neg)], axis=0)
    aa = mm(lhs, rhs, _NT)
    x1 = mm(lhs, h_state)
    vs = stack(v)
    khv = mm(stack(k * e_rest), vs, _TN)
    yield

    row = lax.broadcasted_iota(jnp.int32, (n, n), 0)
    col = lax.broadcasted_iota(jnp.int32, (n, n), 1)
    t_in = row % c
    s_in = col % c
    strict = (s_in > t_in) if reverse else (s_in < t_in)
    incl = (s_in >= t_in) if reverse else (s_in <= t_in)
    n_ab = jnp.where(strict, aa[:n, :n], 0.0)
    a_ak = jnp.where(strict, aa[:n, n:], 0.0)
    a_rb = jnp.where(incl, aa[n:, :n], 0.0)
    a_rk = jnp.where(incl, aa[n:, n:], 0.0)

    eye = (row == col).astype(F32)
    diag = (row // WKV_DIAG) == (col // WKV_DIAG)
    n_d = jnp.where(diag, n_ab, 0.0)
    n_o = jnp.where(diag, 0.0, n_ab)
    x = eye + n_d
    pw = mm(n_d, n_d)
    av = mm(jnp.concatenate([a_ak, a_rk], axis=0), vs)
    yield
    for _ in range(int(math.log2(WKV_DIAG)) - 2):
        x = x + mm(x, pw)
        pw = mm(pw, pw)
        yield
    t_d = x + mm(x, pw)
    yield
    e1 = mm(t_d, n_o)
    yield
    y = eye + e1
    pw = e1
    for _ in range(int(math.log2(c // WKV_DIAG)) - 1):
        pw = mm(pw, pw)
        yield
        y = y + mm(y, pw)
        yield
    t_inv = mm(y, t_d)
    yield
    us = mm(t_inv, x1[:n] + av[:n])
    yield
    os_ = x1[n:] + mm(a_rb, us) + av[n:]
    out = os_[:c] + os_[c:]
    h_new = e_col * h_state + mm(stack(b * e_rest), us, _TN) + khv
    return out, h_new


def _run_lockstep(chains):
    results = [None] * len(chains)
    live = list(range(len(chains)))
    while live:
        still = []
        for i in live:
            try:
                next(chains[i])
                still.append(i)
            except StopIteration as stop:
                results[i] = stop.value
        live = still
    return results


def _wkv_kernel(rf_ref, vf_ref, kkf_ref, lwf_ref, kdf_ref, asf_ref,
                rb_ref, vb_ref, kkb_ref, lwb_ref, kdb_ref, asb_ref,
                of_ref, ob_ref, hf_ref, hb_ref):
    @pl.when(pl.program_id(2) == 0)
    def _():
        hf_ref[...] = jnp.zeros_like(hf_ref)
        hb_ref[...] = jnp.zeros_like(hb_ref)

    chains = []
    for g in range(hf_ref.shape[0]):
        ls = slice(g * LANES, (g + 1) * LANES)
        chains.append(_wkv_chunk(rf_ref[0, :, ls], lwf_ref[0, :, ls], kdf_ref[0, :, ls], vf_ref[0, :, ls],
                                 kkf_ref[0, :, ls], asf_ref[0, :, ls], hf_ref[g], reverse=False))
        chains.append(_wkv_chunk(rb_ref[0, :, ls], lwb_ref[0, :, ls], kdb_ref[0, :, ls], vb_ref[0, :, ls],
                                 kkb_ref[0, :, ls], asb_ref[0, :, ls], hb_ref[g], reverse=True))
    results = _run_lockstep(chains)
    for g in range(hf_ref.shape[0]):
        ls = slice(g * LANES, (g + 1) * LANES)
        (o_f, h_f), (o_b, h_b) = results[2 * g], results[2 * g + 1]
        of_ref[0, :, ls] = o_f
        hf_ref[g] = h_f
        ob_ref[0, :, ls] = o_b
        hb_ref[g] = h_b


def _wkv_scan(r, v, kk, lw_f, kd_f, as_f, lw_b, kd_b, as_b):
    b, t, w = r.shape
    c = WKV_CHUNK
    nc = t // c
    pairs = min(WKV_PAIRS_PER_STEP, w // LANES)
    wb = pairs * LANES
    fwd = pl.BlockSpec((1, c, wb), lambda i, p, s: (i, s, p))
    bwd = pl.BlockSpec((1, c, wb), lambda i, p, s: (i, nc - 1 - s, p))
    out = jax.ShapeDtypeStruct((b, t, w), F32)
    state = pltpu.VMEM((pairs, LANES, LANES), F32)
    return pl.pallas_call(
        _wkv_kernel,
        out_shape=[out, out],
        grid=(b, w // wb, nc),
        in_specs=[fwd] * 6 + [bwd] * 6,
        out_specs=[fwd, bwd],
        scratch_shapes=[state, state],
        compiler_params=_params("parallel", "parallel", "arbitrary"),
        name="wkv_scan",
    )(r, v, kk, lw_f, kd_f, as_f, r, v, kk, lw_b, kd_b, as_b)


def _mla_prep_kernel(x_ref, g_ref, wz_ref, qn_ref, kvn_ref, wqn_ref, wqr_ref, wqs_ref, wkn_ref, wv_ref,
                     qg_ref, qgs_ref, kg_ref, kgs_ref, cos_ref, sin_ref, q_out, k_out, v_out):
    xn = _rms(x_ref[0], g_ref[...]).astype(BF16)
    z = jnp.dot(xn, wz_ref[...], preferred_element_type=F32)
    cq = _rms(z[:, :Q_LORA], qn_ref[...]).astype(BF16)
    ckv = _rms(z[:, Q_LORA:Q_LORA + KV_LORA], kvn_ref[...]).astype(BF16)
    kr = z[:, MLA_IN - QK_ROPE:MLA_IN]
    krs = z[:, MLA_IN:MLA_COLS]
    q_nope = jnp.dot(cq, wqn_ref[...], preferred_element_type=F32)
    q_rope = jnp.dot(cq, wqr_ref[...], preferred_element_type=F32)
    q_swap = jnp.dot(cq, wqs_ref[...], preferred_element_type=F32)
    k_nope = jnp.dot(ckv, wkn_ref[...], preferred_element_type=F32)
    val = jnp.dot(ckv, wv_ref[...], preferred_element_type=F32)
    cos = cos_ref[...]
    sin = sin_ref[...]
    scale = 1.0 / math.sqrt(QK_DIM)
    kr_ss = jnp.sum(kr * kr, axis=-1, keepdims=True)
    for h in range(MLA_HEADS):
        nope = slice(h * QK_NOPE, (h + 1) * QK_NOPE)
        rope = slice(h * QK_ROPE, (h + 1) * QK_ROPE)
        qn, qr, qs = q_nope[:, nope], q_rope[:, rope], q_swap[:, rope]
        ss = jnp.sum(qn * qn, axis=-1, keepdims=True) + jnp.sum(qr * qr, axis=-1, keepdims=True)
        rs = lax.rsqrt(ss / QK_DIM + NORM_EPS)
        q_out[0, h, :, :QK_NOPE] = (qn * rs * qg_ref[:, :QK_NOPE] * scale).astype(BF16)
        rot = qr * rs * qg_ref[:, QK_NOPE:] * cos + qs * rs * qgs_ref[...] * sin
        q_out[0, h, :, QK_NOPE:] = (rot * scale).astype(BF16)
        kn = k_nope[:, nope]
        ss = jnp.sum(kn * kn, axis=-1, keepdims=True) + kr_ss
        rs = lax.rsqrt(ss / QK_DIM + NORM_EPS)
        k_out[0, h, :, :QK_NOPE] = (kn * rs * kg_ref[:, :QK_NOPE]).astype(BF16)
        rot = kr * rs * kg_ref[:, QK_NOPE:] * cos + krs * rs * kgs_ref[...] * sin
        k_out[0, h, :, QK_NOPE:] = rot.astype(BF16)
        v_out[0, h] = val[:, nope].astype(BF16)


def _mla_prep(x, g, wz, q_norm, kv_norm, wqn, wqr, wqs, wkn, wv, qg, qgs, kg, kgs, cos2, sin2, *, tm=256):
    b, t, d = x.shape
    h = MLA_HEADS
    return pl.pallas_call(
        _mla_prep_kernel,
        out_shape=[
            jax.ShapeDtypeStruct((b, h, t, QK_DIM), BF16),
            jax.ShapeDtypeStruct((b, h, t, QK_DIM), BF16),
            jax.ShapeDtypeStruct((b, h, t, V_HEAD), BF16),
        ],
        grid=(b, t // tm),
        in_specs=[
            pl.BlockSpec((1, tm, d), lambda i, j: (i, j, 0)),
            _const_spec((1, d)),
            _const_spec(wz.shape),
            _const_spec((1, Q_LORA)),
            _const_spec((1, KV_LORA)),
            _const_spec(wqn.shape),
            _const_spec(wqr.shape),
            _const_spec(wqs.shape),
            _const_spec(wkn.shape),
            _const_spec(wv.shape),
            _const_spec((1, QK_DIM)),
            _const_spec((1, QK_ROPE)),
            _const_spec((1, QK_DIM)),
            _const_spec((1, QK_ROPE)),
            pl.BlockSpec((tm, QK_ROPE), lambda i, j: (j, 0)),
            pl.BlockSpec((tm, QK_ROPE), lambda i, j: (j, 0)),
        ],
        out_specs=[
            pl.BlockSpec((1, h, tm, QK_DIM), lambda i, j: (i, 0, j, 0)),
            pl.BlockSpec((1, h, tm, QK_DIM), lambda i, j: (i, 0, j, 0)),
            pl.BlockSpec((1, h, tm, V_HEAD), lambda i, j: (i, 0, j, 0)),
        ],
        compiler_params=_params("parallel", "parallel"),
        name="mla_prep",
    )(x, g, wz, q_norm, kv_norm, wqn, wqr, wqs, wkn, wv, qg, qgs, kg, kgs, cos2, sin2)


def _attn_kernel(q_ref, k_ref, v_ref, o_ref, s_scr, p_scr, m_ref, l_ref, acc_ref, *, sub):
    n_sub = k_ref.shape[2] // sub
    assert n_sub == 1 or n_sub % 2 == 0
    q = q_ref[0, 0]

    def scores(j, slot):
        rows = pl.ds(pl.multiple_of(j * sub, sub), sub)
        s_scr[slot] = lax.dot_general(q, k_ref[0, 0, rows, :], _NT, preferred_element_type=F32)

    def weighted(j, slot):
        rows = pl.ds(pl.multiple_of(j * sub, sub), sub)
        return jnp.dot(p_scr[slot], v_ref[0, 0, rows, :], preferred_element_type=F32)

    def softmax(slot):
        s = s_scr[slot]
        m_old = m_ref[...]
        m_new = jnp.maximum(m_old, jnp.max(s, axis=-1, keepdims=True))
        alpha = jnp.exp(m_old - m_new)
        p = jnp.exp(s - m_new)
        l_ref[...] = alpha * l_ref[...] + jnp.sum(p, axis=-1, keepdims=True)
        m_ref[...] = m_new
        p_scr[slot] = p.astype(BF16)
        return alpha

    def stage(j, slot, first=False, last=False):
        if not last:
            scores(j + 1, 1 - slot)
        pv = None if first else weighted(j - 1, 1 - slot)
        alpha = softmax(slot)
        if not first:
            acc_ref[...] = alpha * (acc_ref[...] + pv)

    m_ref[...] = jnp.full_like(m_ref, -jnp.inf)
    l_ref[...] = jnp.zeros_like(l_ref)
    acc_ref[...] = jnp.zeros_like(acc_ref)
    scores(0, 0)
    if n_sub == 1:
        stage(0, 0, first=True, last=True)
    else:
        stage(0, 0, first=True)

        def two_stages(i, carry):
            stage(2 * i + 1, 1)
            stage(2 * i + 2, 0)
            return carry

        lax.fori_loop(0, (n_sub - 2) // 2, two_stages, 0)
        stage(n_sub - 1, 1, last=True)
    acc = acc_ref[...] + weighted(n_sub - 1, (n_sub - 1) % 2)
    o_ref[0] = (acc / l_ref[...]).astype(o_ref.dtype)


def _attention(q, k, v, *, tq=512, sub=1024):
    b, h, t, dq = q.shape
    dv = v.shape[-1]
    tq = min(tq, t)
    sub = min(sub, t)
    return pl.pallas_call(
        functools.partial(_attn_kernel, sub=sub),
        out_shape=jax.ShapeDtypeStruct((b, t, h * dv), BF16),
        grid=(b, h, t // tq),
        in_specs=[
            pl.BlockSpec((1, 1, tq, dq), lambda i, j, qi: (i, j, qi, 0)),
            pl.BlockSpec((1, 1, t, dq), lambda i, j, qi: (i, j, 0, 0)),
            pl.BlockSpec((1, 1, t, dv), lambda i, j, qi: (i, j, 0, 0)),
        ],
        out_specs=pl.BlockSpec((1, tq, dv), lambda i, j, qi: (i, qi, j)),
        scratch_shapes=[
            pltpu.VMEM((2, tq, sub), F32),
            pltpu.VMEM((2, tq, sub), BF16),
            pltpu.VMEM((tq, 1), F32),
            pltpu.VMEM((tq, 1), F32),
            pltpu.VMEM((tq, dv), F32),
        ],
        compiler_params=_params("parallel", "parallel", "arbitrary"),
        name="mla_attention",
    )(q, k, v)


def _group_norm(x, g, b):
    mean = _head_sum(x) * (1.0 / RWKV_HEAD_DIM)
    xc = x - mean
    var = _head_sum(xc * xc) * (1.0 / RWKV_HEAD_DIM)
    return xc * lax.rsqrt(var + GN_EPS) * g + b


def _merge_kernel(x_ref, wf_ref, wb_ref, bonus_ref, g_ref, ob_ref, za_ref, zb_ref,
                  lng_ref, lnb_ref, wa_ref, wbr_ref, wo_ref, o_ref):
    lng = lng_ref[...]
    lnb = lnb_ref[...]
    o_a = (_group_norm(wf_ref[...], lng, lnb) + _group_norm(wb_ref[...], lng, lnb) + bonus_ref[...]) * g_ref[...]
    br_a = jnp.dot(o_a.astype(BF16), wa_ref[...], preferred_element_type=F32)
    br_b = jnp.dot(ob_ref[...], wbr_ref[...], preferred_element_type=F32)
    merged = _sigmoid(za_ref[...]) * br_a + _sigmoid(zb_ref[...]) * br_b
    o_ref[...] = x_ref[...] + jnp.dot(merged.astype(BF16), wo_ref[...], preferred_element_type=F32)


def _merge(x, wkv_f, wkv_b, bonus, g, o_b, z_gate, ln_g, ln_b, w_a, w_b, w_o, *, tm=256):
    m, d = x.shape
    w = RWKV_WIDTH
    row_w = pl.BlockSpec((tm, w), lambda i: (i, 0))
    return pl.pallas_call(
        _merge_kernel,
        out_shape=jax.ShapeDtypeStruct((m, d), F32),
        grid=(m // tm,),
        in_specs=[
            pl.BlockSpec((tm, d), lambda i: (i, 0)),
            row_w, row_w, row_w, row_w, row_w,
            pl.BlockSpec((tm, d), lambda i: (i, 0)),
            pl.BlockSpec((tm, d), lambda i: (i, 1)),
            _const_spec((1, w)),
            _const_spec((1, w)),
            _const_spec(w_a.shape),
            _const_spec(w_b.shape),
            _const_spec(w_o.shape),
        ],
        out_specs=pl.BlockSpec((tm, d), lambda i: (i, 0)),
        compiler_params=_params("parallel"),
        name="merge",
    )(x, wkv_f, wkv_b, bonus, g, o_b, z_gate, z_gate, ln_g, ln_b, w_a, w_b, w_o)


def _swap_halves(x, axis=-1):
    lo, hi = jnp.split(x, 2, axis=axis)
    return jnp.concatenate([hi, lo], axis=axis)


def _prepare_weights(p):
    w = {}
    w['ffn1_w_in'] = p['ffn1_w_in'].astype(BF16)
    w['ffn1_w_out'] = p['ffn1_w_out'].astype(BF16)
    w['ffn2_w_in'] = p['ffn2_w_in'].astype(BF16)
    w['ffn2_w_out'] = p['ffn2_w_out'].astype(BF16)
    w_in = p['w_in']
    w['w_rwkv'] = w_in[:, :RWKV_IN].astype(BF16)
    mla = w_in[:, RWKV_IN:RWKV_IN + MLA_IN]
    w['w_mla'] = jnp.concatenate([mla, _swap_halves(mla[:, MLA_IN - QK_ROPE:])], axis=1).astype(BF16)
    w['w_gate'] = w_in[:, RWKV_IN + MLA_IN:].astype(BF16)

    def lora_rows(mat, slot):
        z = jnp.zeros((LORA_SLAB, RWKV_WIDTH), F32)
        return lax.dynamic_update_slice(z, mat, (slot * DECAY_LORA, 0)).astype(BF16)

    w['w2f'] = lora_rows(p['rwkv_w2'][0], 0)
    w['w2b'] = lora_rows(p['rwkv_w2'][1], 1)
    w['a2f'] = lora_rows(p['rwkv_a2'][0], 2)
    w['a2b'] = lora_rows(p['rwkv_a2'][1], 3)
    w['g2'] = p['rwkv_g2'].astype(BF16)

    w_uq = p['mla_w_uq'].reshape(Q_LORA, MLA_HEADS, QK_DIM)
    w['wqn'] = w_uq[:, :, :QK_NOPE].reshape(Q_LORA, MLA_HEADS * QK_NOPE).astype(BF16)
    wqr = w_uq[:, :, QK_NOPE:]
    w['wqr'] = wqr.reshape(Q_LORA, MLA_HEADS * QK_ROPE).astype(BF16)
    w['wqs'] = _swap_halves(wqr).reshape(Q_LORA, MLA_HEADS * QK_ROPE).astype(BF16)
    w_ukv = p['mla_w_ukv'].reshape(KV_LORA, MLA_HEADS, QK_NOPE + V_HEAD)
    w['wkn'] = w_ukv[:, :, :QK_NOPE].reshape(KV_LORA, MLA_HEADS * QK_NOPE).astype(BF16)
    w['wv'] = w_ukv[:, :, QK_NOPE:].reshape(KV_LORA, MLA_HEADS * V_HEAD).astype(BF16)
    w['qgs'] = _swap_halves(p['mla_q_gain'][QK_NOPE:])[None, :]
    w['kgs'] = _swap_halves(p['mla_k_gain'][QK_NOPE:])[None, :]
    w['w_a'] = p['w_branch_a'].astype(BF16)
    w['w_b'] = p['w_branch_b'].astype(BF16)
    w['w_o'] = p['w_out'].astype(BF16)
    return w


def _rope_tables(t):
    inv = ROPE_BASE ** (-jnp.arange(0, QK_ROPE, 2, dtype=F32) / QK_ROPE)
    ang = jnp.arange(t, dtype=F32)[:, None] * inv[None, :]
    cos, sin = jnp.cos(ang), jnp.sin(ang)
    return jnp.concatenate([cos, cos], axis=1), jnp.concatenate([-sin, sin], axis=1)


def _row(vec):
    return vec.reshape(1, -1)


def _encoder_layer(x, p, w):
    b, t, d = x.shape
    m = b * t
    x0 = x.reshape(m, d)
    x1 = _ffn(x0, _row(p['ffn1_norm']), w['ffn1_w_in'], w['ffn1_w_out'])
    mix_g = _row(p['mix_norm'])
    z_rwkv = _norm_matmul(x1, mix_g, w['w_rwkv'], tm=256, tn=RWKV_IN, name="in_proj_rwkv")
    z_gate = _norm_matmul(x1, mix_g, w['w_gate'], tm=512, tn=1024, name="in_proj_gate")

    (r, v, kk, g, bonus, lw_f, kd_f, as_f, lw_b, kd_b, as_b) = _rwkv_prep(
        z_rwkv.reshape(b, t, RWKV_IN), _row(p['rwkv_mu']), p['rwkv_w0'], p['rwkv_a0'],
        w['w2f'], w['w2b'], w['a2f'], w['a2b'], w['g2'],
        _row(p['rwkv_k_k']), _row(p['rwkv_k_a']), _row(p['rwkv_r_k']))
    wkv_f, wkv_b = _wkv_scan(r, v, kk, lw_f, kd_f, as_f, lw_b, kd_b, as_b)

    cos2, sin2 = _rope_tables(t)
    q, k, val = _mla_prep(
        x1.reshape(b, t, d), mix_g, w['w_mla'], _row(p['mla_q_norm']), _row(p['mla_kv_norm']),
        w['wqn'], w['wqr'], w['wqs'], w['wkn'], w['wv'],
        _row(p['mla_q_gain']), w['qgs'], _row(p['mla_k_gain']), w['kgs'], cos2, sin2)
    o_b = _attention(q, k, val)

    flat = lambda u: u.reshape(m, -1)
    x2 = _merge(x1, flat(wkv_f), flat(wkv_b), flat(bonus), flat(g), flat(o_b), z_gate,
                _row(p['rwkv_ln_g']), _row(p['rwkv_ln_b']), w['w_a'], w['w_b'], w['w_o'])
    y = _ffn(x2, _row(p['ffn2_norm']), w['ffn2_w_in'], w['ffn2_w_out'], _row(p['out_norm']))
    return y.reshape(b, t, d)


def kernel(x_prompt, x_sample, ffn1_norm, ffn1_w_in, ffn1_w_out, mix_norm, w_in, rwkv_mu, rwkv_w0, rwkv_w2, rwkv_a0, rwkv_a2, rwkv_g2, rwkv_k_k, rwkv_k_a, rwkv_r_k, rwkv_ln_g, rwkv_ln_b, mla_q_norm, mla_w_uq, mla_kv_norm, mla_w_ukv, mla_q_gain, mla_k_gain, w_branch_a, w_branch_b, w_out, ffn2_norm, ffn2_w_in, ffn2_w_out, out_norm):
    stacked = dict(
        ffn1_norm=ffn1_norm, ffn1_w_in=ffn1_w_in, ffn1_w_out=ffn1_w_out, mix_norm=mix_norm, w_in=w_in,
        rwkv_mu=rwkv_mu, rwkv_w0=rwkv_w0, rwkv_w2=rwkv_w2, rwkv_a0=rwkv_a0, rwkv_a2=rwkv_a2,
        rwkv_g2=rwkv_g2, rwkv_k_k=rwkv_k_k, rwkv_k_a=rwkv_k_a, rwkv_r_k=rwkv_r_k,
        rwkv_ln_g=rwkv_ln_g, rwkv_ln_b=rwkv_ln_b, mla_q_norm=mla_q_norm, mla_w_uq=mla_w_uq,
        mla_kv_norm=mla_kv_norm, mla_w_ukv=mla_w_ukv, mla_q_gain=mla_q_gain, mla_k_gain=mla_k_gain,
        w_branch_a=w_branch_a, w_branch_b=w_branch_b, w_out=w_out, ffn2_norm=ffn2_norm,
        ffn2_w_in=ffn2_w_in, ffn2_w_out=ffn2_w_out, out_norm=out_norm)
    y_prompt, y_sample = x_prompt, x_sample
    for layer in range(ffn1_norm.shape[0]):
        p = {name: arr[layer] for name, arr in stacked.items()}
        p['rwkv_r_k'] = p['rwkv_r_k'].reshape(-1)
        w = _prepare_weights(p)
        y_prompt = _encoder_layer(y_prompt, p, w)
        y_sample = _encoder_layer(y_sample, p, w)
    return (y_prompt, y_sample)
```

```python
import functools
import math

import jax
import jax.numpy as jnp
from jax import lax
from jax.experimental import pallas as pl
from jax.experimental.pallas import tpu as pltpu

F32 = jnp.float32
BF16 = jnp.bfloat16

D_MODEL = 2048
D_FF = 5632
NORM_EPS = 1e-6

RWKV_HEADS = 16
RWKV_HEAD_DIM = 64
RWKV_WIDTH = RWKV_HEADS * RWKV_HEAD_DIM
DECAY_LORA = 96
AAA_LORA = 96
GATE_LORA = 256
LORA_SLAB = 2 * DECAY_LORA + 2 * AAA_LORA
GN_EPS = 64e-5
RWKV_IN = 3 * RWKV_WIDTH + LORA_SLAB + GATE_LORA

MLA_HEADS = 8
Q_LORA = 512
KV_LORA = 512
QK_NOPE = 128
QK_ROPE = 64
QK_DIM = QK_NOPE + QK_ROPE
V_HEAD = 128
MLA_WIDTH = MLA_HEADS * V_HEAD
MLA_IN = Q_LORA + KV_LORA + QK_ROPE
MLA_COLS = MLA_IN + QK_ROPE
ROPE_BASE = 10000.0
GATE_IN = 2 * D_MODEL

LANES = 128
SUBLANES = 8
MXU_DIM = 256
VMEM_LIMIT_BYTES = 56 * 1024 * 1024

WKV_CHUNK = 64
WKV_DIAG = 16
WKV_PARTS = 1
WKV_PAIRS_PER_STEP = 8


def _params(*semantics):
    return pltpu.CompilerParams(dimension_semantics=semantics, vmem_limit_bytes=VMEM_LIMIT_BYTES)


def _const_spec(shape):
    zeros = (0,) * len(shape)
    return pl.BlockSpec(shape, lambda *_: zeros, pipeline_mode=pl.Buffered(1))


def _rms(x, g):
    ms = jnp.mean(x * x, axis=-1, keepdims=True)
    return x * lax.rsqrt(ms + NORM_EPS) * g


def _sigmoid(x):
    return 0.5 * jnp.tanh(0.5 * x) + 0.5


def _bf16_parts(x, n):
    if x.dtype == BF16:
        return [x]
    parts = []
    rem = x
    for i in range(n):
        p = rem.astype(BF16)
        parts.append(p)
        if i + 1 < n:
            rem = rem - p.astype(F32)
    return parts


_NN = (((1,), (0,)), ((), ()))
_NT = (((1,), (1,)), ((), ()))
_TN = (((0,), (0,)), ((), ()))


def _dotp(a, b, dims=_NN, pa=1, pb=1):
    a_parts = _bf16_parts(a, pa)
    b_parts = _bf16_parts(b, pb)
    order = max(len(a_parts), len(b_parts)) - 1
    acc = None
    for i, ap in enumerate(a_parts):
        for j, bp in enumerate(b_parts):
            if i + j > order:
                continue
            t = lax.dot_general(ap, bp, dims, preferred_element_type=F32)
            acc = t if acc is None else acc + t
    return acc


def _ffn_kernel(x_ref, g_ref, wg_ref, wu_ref, wo_ref, *rest, final_norm):
    if final_norm:
        gf_ref, o_ref, xn_ref = rest
    else:
        o_ref, xn_ref = rest
    j = pl.program_id(1)

    @pl.when(j == 0)
    def _():
        xn_ref[...] = _rms(x_ref[...], g_ref[...]).astype(BF16)
        o_ref[...] = jnp.zeros_like(o_ref)

    xn = xn_ref[...]
    gate = jnp.dot(xn, wg_ref[...], preferred_element_type=F32)
    up = jnp.dot(xn, wu_ref[...], preferred_element_type=F32)
    h = (gate * _sigmoid(gate) * up).astype(BF16)
    o_ref[...] += jnp.dot(h, wo_ref[...], preferred_element_type=F32)

    @pl.when(j == pl.num_programs(1) - 1)
    def _():
        y = x_ref[...] + 0.5 * o_ref[...]
        if final_norm:
            y = _rms(y, gf_ref[...])
        o_ref[...] = y


def _ffn(x, g, w_in, w_out, g_final=None, *, tm=512, tf=512):
    m, d = x.shape
    n_ff = w_out.shape[0]
    nj = n_ff // tf
    in_specs = [
        pl.BlockSpec((tm, d), lambda i, j: (i, 0)),
        pl.BlockSpec((1, d), lambda i, j: (0, 0)),
        pl.BlockSpec((d, tf), lambda i, j: (0, j)),
        pl.BlockSpec((d, tf), lambda i, j: (0, j + nj)),
        pl.BlockSpec((tf, d), lambda i, j: (j, 0)),
    ]
    args = [x, g, w_in, w_in, w_out]
    if g_final is not None:
        in_specs.append(pl.BlockSpec((1, d), lambda i, j: (0, 0)))
        args.append(g_final)
    return pl.pallas_call(
        functools.partial(_ffn_kernel, final_norm=g_final is not None),
        out_shape=jax.ShapeDtypeStruct((m, d), F32),
        grid=(m // tm, nj),
        in_specs=in_specs,
        out_specs=pl.BlockSpec((tm, d), lambda i, j: (i, 0)),
        scratch_shapes=[pltpu.VMEM((tm, d), BF16)],
        compiler_params=_params("parallel", "arbitrary"),
        name="ffn",
    )(*args)


def _norm_matmul_kernel(x_ref, g_ref, w_ref, o_ref, xn_ref, *, gate):
    @pl.when(pl.program_id(1) == 0)
    def _():
        xn_ref[...] = _rms(x_ref[...], g_ref[...]).astype(BF16)

    z = jnp.dot(xn_ref[...], w_ref[...], preferred_element_type=F32)
    o_ref[...] = (_sigmoid(z) if gate else z).astype(o_ref.dtype)


def _norm_matmul(x, g, w, *, tm, tn, name, gate=False):
    m, d = x.shape
    n = w.shape[1]
    w_spec = _const_spec((d, n)) if tn == n else pl.BlockSpec((d, tn), lambda i, j: (0, j))
    return pl.pallas_call(
        functools.partial(_norm_matmul_kernel, gate=gate),
        out_shape=jax.ShapeDtypeStruct((m, n), BF16 if gate else F32),
        grid=(m // tm, n // tn),
        in_specs=[
            pl.BlockSpec((tm, d), lambda i, j: (i, 0)),
            pl.BlockSpec((1, d), lambda i, j: (0, 0)),
            w_spec,
        ],
        out_specs=pl.BlockSpec((tm, tn), lambda i, j: (i, j)),
        scratch_shapes=[pltpu.VMEM((tm, d), BF16)],
        compiler_params=_params("parallel", "arbitrary"),
        name=name,
    )(x, g, w)


def _head_sum(x):
    wide = MXU_DIM
    r = lax.broadcasted_iota(jnp.int32, (wide, wide), 0) // RWKV_HEAD_DIM
    c = lax.broadcasted_iota(jnp.int32, (wide, wide), 1) // RWKV_HEAD_DIM
    ones = (r == c).astype(BF16)
    tiles = [
        _dotp(x[:, j * wide:(j + 1) * wide], ones, pa=1, pb=1)
        for j in range(x.shape[1] // wide)
    ]
    return jnp.concatenate(tiles, axis=1)


def _rwkv_prep_kernel(z_ref, zp_ref, zn_ref, mu_ref, w0_ref, a0_ref, w2f_ref, w2b_ref, a2f_ref,
                      a2b_ref, g2_ref, kk_ref, ka_ref, rk_ref,
                      r_out, v_out, kkn_out, g_out, bonus_out,
                      lwf_out, kdf_out, asf_out, lwb_out, kdb_out, asb_out):
    j = pl.program_id(1)
    nj = pl.num_programs(1)
    z = z_ref[0]
    tm = z.shape[0]
    row = lax.broadcasted_iota(jnp.int32, z.shape, 0)
    halo_prev = jnp.where(j > 0, zp_ref[0, SUBLANES - 1:SUBLANES, :], 0.0)
    halo_next = jnp.where(j < nj - 1, zn_ref[0, 0:1, :], 0.0)
    prev = jnp.where(row == 0, halo_prev, pltpu.roll(z, 1, axis=0))
    nxt = jnp.where(row == tm - 1, halo_next, pltpu.roll(z, tm - 1, axis=0))
    zs = z + mu_ref[...] * (0.5 * (prev + nxt) - z)

    w = RWKV_WIDTH
    r = zs[:, 0:w]
    k = zs[:, w:2 * w]
    v = zs[:, 2 * w:3 * w]
    lora = zs[:, 3 * w:3 * w + LORA_SLAB]
    gd = zs[:, 3 * w + LORA_SLAB:]

    g_out[0] = jnp.dot(_sigmoid(gd).astype(BF16), g2_ref[...], preferred_element_type=F32).astype(g_out.dtype)
    kk = k * kk_ref[...]
    kk = kk * lax.rsqrt(_head_sum(kk * kk) + 1e-12)
    r_out[0] = r.astype(r_out.dtype)
    v_out[0] = v.astype(v_out.dtype)
    kkn_out[0] = kk.astype(kkn_out.dtype)

    lora_t = jnp.tanh(lora).astype(BF16)
    lora_l = lora.astype(BF16)
    bonus = None
    for d, (w2_ref, a2_ref, lw_out, kd_out, as_out) in enumerate(
            ((w2f_ref, a2f_ref, lwf_out, kdf_out, asf_out), (w2b_ref, a2b_ref, lwb_out, kdb_out, asb_out))):
        w_pre = w0_ref[d:d + 1, :] + jnp.dot(lora_t, w2_ref[...], preferred_element_type=F32)
        lw_out[0] = -math.exp(-0.5) * _sigmoid(w_pre)
        a = _sigmoid(a0_ref[d:d + 1, :] + jnp.dot(lora_l, a2_ref[...], preferred_element_type=F32))
        k_d = k * (1.0 + (a - 1.0) * ka_ref[...])
        kd_out[0] = k_d.astype(kd_out.dtype)
        as_out[0] = a.astype(as_out.dtype)
        b_d = _head_sum(r * k_d * rk_ref[...]) * v
        bonus = b_d if bonus is None else bonus + b_d
    bonus_out[0] = bonus.astype(bonus_out.dtype)


def _rwkv_prep(z, mu, w0, a0, w2f, w2b, a2f, a2b, g2, k_k, k_a, r_k, *, tm=256):
    b, t, zc = z.shape
    w = RWKV_WIDTH
    nb = tm // SUBLANES
    last8 = t // SUBLANES - 1
    narrow = jax.ShapeDtypeStruct((b, t, w), BF16)
    wide = jax.ShapeDtypeStruct((b, t, w), F32)
    row_spec = pl.BlockSpec((1, tm, w), lambda i, j: (i, j, 0))
    return pl.pallas_call(
        _rwkv_prep_kernel,
        out_shape=[narrow] * 5 + [wide, narrow, narrow] * 2,
        grid=(b, t // tm),
        in_specs=[
            pl.BlockSpec((1, tm, zc), lambda i, j: (i, j, 0)),
            pl.BlockSpec((1, SUBLANES, zc), lambda i, j: (i, jnp.maximum(j * nb - 1, 0), 0)),
            pl.BlockSpec((1, SUBLANES, zc), lambda i, j: (i, jnp.minimum((j + 1) * nb, last8), 0)),
            _const_spec((1, zc)),
            _const_spec((2, w)),
            _const_spec((2, w)),
            _const_spec((LORA_SLAB, w)),
            _const_spec((LORA_SLAB, w)),
            _const_spec((LORA_SLAB, w)),
            _const_spec((LORA_SLAB, w)),
            _const_spec((GATE_LORA, w)),
            _const_spec((1, w)),
            _const_spec((1, w)),
            _const_spec((1, w)),
        ],
        out_specs=[row_spec] * 11,
        compiler_params=_params("parallel", "parallel"),
        name="rwkv_prep",
    )(z, z, z, mu, w0, a0, w2f, w2b, a2f, a2b, g2, k_k, k_a, r_k)


def _wkv_chunk(r, lw, k, v, kk, asig, h_state, reverse):
    c = r.shape[0]
    n = 2 * c
    mm = functools.partial(_dotp, pa=WKV_PARTS, pb=WKV_PARTS)
    a = -kk
    b = kk * asig

    ti = lax.broadcasted_iota(jnp.int32, (c, c), 0)
    si = lax.broadcasted_iota(jnp.int32, (c, c), 1)
    tri = ((si >= ti) if reverse else (si <= ti)).astype(BF16)
    cum = _dotp(tri, lw, pa=1, pb=3)
    yield

    cum_x = cum - lw
    total = jnp.sum(lw, axis=0, keepdims=True)
    e_in = jnp.exp(cum)
    e_ex = jnp.exp(cum_x)
    e_neg = jnp.exp(-cum)
    e_rest = jnp.exp(total - cum)

    lane = lax.broadcasted_iota(jnp.int32, (c, LANES), 1)
    first = lane < RWKV_HEAD_DIM

    def stack(x):
        return jnp.concatenate([jnp.where(first, x, 0.0), jnp.where(first, 0.0, x)], axis=0)

    lhs = jnp.concatenate([stack(a * e_ex), stack(r * e_in)], axis=0)
    rhs = jnp.concatenate([stack(b * e_neg), stack(k * e_neg)], axis=0)
    aa = mm(lhs, rhs, _NT)
    x1 = mm(lhs, h_state, _NT)
    vs = stack(v)
    khv = mm(vs, stack(k * e_rest), _TN)
    yield

    row = lax.broadcasted_iota(jnp.int32, (n, n), 0)
    col = lax.broadcasted_iota(jnp.int32, (n, n), 1)
    t_in = row % c
    s_in = col % c
    strict = (s_in > t_in) if reverse else (s_in < t_in)
    incl = (s_in >= t_in) if reverse else (s_in <= t_in)
    n_ab = jnp.where(strict, aa[:n, :n], 0.0)
    a_ak = jnp.where(strict, aa[:n, n:], 0.0)
    a_rb = jnp.where(incl, aa[n:, :n], 0.0)
    a_rk = jnp.where(incl, aa[n:, n:], 0.0)

    eye = (row == col).astype(F32)
    diag = (row // WKV_DIAG) == (col // WKV_DIAG)
    n_d = jnp.where(diag, n_ab, 0.0)
    n_o = jnp.where(diag, 0.0, n_ab)
    x = eye + n_d
    pw = mm(n_d, n_d)
    av = mm(jnp.concatenate([a_ak, a_rk], axis=0), vs)
    yield
    for _ in range(int(math.log2(WKV_DIAG)) - 2):
        x = x + mm(x, pw)
        pw = mm(pw, pw)
        yield
    t_d = x + mm(x, pw)
    yield
    e1 = mm(t_d, n_o)
    yield
    y = eye + e1
    pw = e1
    for _ in range(int(math.log2(c // WKV_DIAG)) - 1):
        pw = mm(pw, pw)
        yield
        y = y + mm(y, pw)
        yield
    t_inv = mm(y, t_d)
    yield
    us = mm(t_inv, x1[:n] + av[:n])
    yield
    os_ = x1[n:] + mm(a_rb, us) + av[n:]
    out = os_[:c] + os_[c:]
    h_new = jnp.exp(total) * h_state + mm(us, stack(b * e_rest), _TN) + khv
    return out, h_new


def _run_lockstep(chains):
    results = [None] * len(chains)
    live = list(range(len(chains)))
    while live:
        still = []
        for i in live:
            try:
                next(chains[i])
                still.append(i)
            except StopIteration as stop:
                results[i] = stop.value
        live = still
    return results


def _wkv_kernel(rf_ref, vf_ref, kkf_ref, lwf_ref, kdf_ref, asf_ref,
                rb_ref, vb_ref, kkb_ref, lwb_ref, kdb_ref, asb_ref,
                of_ref, ob_ref, hf_ref, hb_ref):
    @pl.when(pl.program_id(2) == 0)
    def _():
        hf_ref[...] = jnp.zeros_like(hf_ref)
        hb_ref[...] = jnp.zeros_like(hb_ref)

    chains = []
    for g in range(hf_ref.shape[0]):
        ls = slice(g * LANES, (g + 1) * LANES)
        fwd = [ref[0, :, ls].astype(F32) for ref in (rf_ref, lwf_ref, kdf_ref, vf_ref, kkf_ref, asf_ref)]
        bwd = [ref[0, :, ls].astype(F32) for ref in (rb_ref, lwb_ref, kdb_ref, vb_ref, kkb_ref, asb_ref)]
        chains.append(_wkv_chunk(*fwd, hf_ref[g], reverse=False))
        chains.append(_wkv_chunk(*bwd, hb_ref[g], reverse=True))
    results = _run_lockstep(chains)
    for g in range(hf_ref.shape[0]):
        ls = slice(g * LANES, (g + 1) * LANES)
        (o_f, h_f), (o_b, h_b) = results[2 * g], results[2 * g + 1]
        of_ref[0, :, ls] = o_f.astype(of_ref.dtype)
        hf_ref[g] = h_f
        ob_ref[0, :, ls] = o_b.astype(ob_ref.dtype)
        hb_ref[g] = h_b


def _wkv_scan(r, v, kk, lw_f, kd_f, as_f, lw_b, kd_b, as_b):
    b, t, w = r.shape
    c = WKV_CHUNK
    nc = t // c
    pairs = min(WKV_PAIRS_PER_STEP, w // LANES)
    wb = pairs * LANES
    fwd = pl.BlockSpec((1, c, wb), lambda i, p, s: (i, s, p))
    bwd = pl.BlockSpec((1, c, wb), lambda i, p, s: (i, nc - 1 - s, p))
    out = jax.ShapeDtypeStruct((b, t, w), BF16)
    state = pltpu.VMEM((pairs, LANES, LANES), F32)
    return pl.pallas_call(
        _wkv_kernel,
        out_shape=[out, out],
        grid=(b, w // wb, nc),
        in_specs=[fwd] * 6 + [bwd] * 6,
        out_specs=[fwd, bwd],
        scratch_shapes=[state, state],
        compiler_params=_params("parallel", "parallel", "arbitrary"),
        name="wkv_scan",
    )(r, v, kk, lw_f, kd_f, as_f, r, v, kk, lw_b, kd_b, as_b)


def _mla_prep_kernel(x_ref, g_ref, wz_ref, qn_ref, kvn_ref, wqn_ref, wqr_ref, wqs_ref, wkn_ref, wv_ref,
                     qg_ref, qgs_ref, kg_ref, kgs_ref, cos_ref, sin_ref, q_out, k_out, v_out):
    xn = _rms(x_ref[0], g_ref[...]).astype(BF16)
    z = jnp.dot(xn, wz_ref[...], preferred_element_type=F32)
    cq = _rms(z[:, :Q_LORA], qn_ref[...]).astype(BF16)
    ckv = _rms(z[:, Q_LORA:Q_LORA + KV_LORA], kvn_ref[...]).astype(BF16)
    kr = z[:, MLA_IN - QK_ROPE:MLA_IN]
    krs = z[:, MLA_IN:MLA_COLS]
    q_nope = jnp.dot(cq, wqn_ref[...], preferred_element_type=F32)
    q_rope = jnp.dot(cq, wqr_ref[...], preferred_element_type=F32)
    q_swap = jnp.dot(cq, wqs_ref[...], preferred_element_type=F32)
    k_nope = jnp.dot(ckv, wkn_ref[...], preferred_element_type=F32)
    val = jnp.dot(ckv, wv_ref[...], preferred_element_type=F32)
    cos = cos_ref[...]
    sin = sin_ref[...]
    scale = math.log2(math.e) / math.sqrt(QK_DIM)
    ones_v = jnp.ones((val.shape[0], V_HEAD), BF16)
    kr_ss = jnp.sum(kr * kr, axis=-1, keepdims=True)
    for h in range(MLA_HEADS):
        nope = slice(h * QK_NOPE, (h + 1) * QK_NOPE)
        rope = slice(h * QK_ROPE, (h + 1) * QK_ROPE)
        qn, qr, qs = q_nope[:, nope], q_rope[:, rope], q_swap[:, rope]
        ss = jnp.sum(qn * qn, axis=-1, keepdims=True) + jnp.sum(qr * qr, axis=-1, keepdims=True)
        rs = lax.rsqrt(ss / QK_DIM + NORM_EPS)
        q_out[0, h, :, :QK_NOPE] = (qn * rs * qg_ref[:, :QK_NOPE] * scale).astype(BF16)
        rot = qr * rs * qg_ref[:, QK_NOPE:] * cos + qs * rs * qgs_ref[...] * sin
        q_out[0, h, :, QK_NOPE:] = (rot * scale).astype(BF16)
        kn = k_nope[:, nope]
        ss = jnp.sum(kn * kn, axis=-1, keepdims=True) + kr_ss
        rs = lax.rsqrt(ss / QK_DIM + NORM_EPS)
        k_out[0, h, :, :QK_NOPE] = (kn * rs * kg_ref[:, :QK_NOPE]).astype(BF16)
        rot = kr * rs * kg_ref[:, QK_NOPE:] * cos + krs * rs * kgs_ref[...] * sin
        k_out[0, h, :, QK_NOPE:] = rot.astype(BF16)
        v_out[0, h, :, :V_HEAD] = val[:, nope].astype(BF16)
        v_out[0, h, :, V_HEAD:] = ones_v


def _mla_prep(x, g, wz, q_norm, kv_norm, wqn, wqr, wqs, wkn, wv, qg, qgs, kg, kgs, cos2, sin2, *, tm=256):
    b, t, d = x.shape
    h = MLA_HEADS
    return pl.pallas_call(
        _mla_prep_kernel,
        out_shape=[
            jax.ShapeDtypeStruct((b, h, t, QK_DIM), BF16),
            jax.ShapeDtypeStruct((b, h, t, QK_DIM), BF16),
            jax.ShapeDtypeStruct((b, h, t, 2 * V_HEAD), BF16),
        ],
        grid=(b, t // tm),
        in_specs=[
            pl.BlockSpec((1, tm, d), lambda i, j: (i, j, 0)),
            _const_spec((1, d)),
            _const_spec(wz.shape),
            _const_spec((1, Q_LORA)),
            _const_spec((1, KV_LORA)),
            _const_spec(wqn.shape),
            _const_spec(wqr.shape),
            _const_spec(wqs.shape),
            _const_spec(wkn.shape),
            _const_spec(wv.shape),
            _const_spec((1, QK_DIM)),
            _const_spec((1, QK_ROPE)),
            _const_spec((1, QK_DIM)),
            _const_spec((1, QK_ROPE)),
            pl.BlockSpec((tm, QK_ROPE), lambda i, j: (j, 0)),
            pl.BlockSpec((tm, QK_ROPE), lambda i, j: (j, 0)),
        ],
        out_specs=[
            pl.BlockSpec((1, h, tm, QK_DIM), lambda i, j: (i, 0, j, 0)),
            pl.BlockSpec((1, h, tm, QK_DIM), lambda i, j: (i, 0, j, 0)),
            pl.BlockSpec((1, h, tm, 2 * V_HEAD), lambda i, j: (i, 0, j, 0)),
        ],
        compiler_params=_params("parallel", "parallel"),
        name="mla_prep",
    )(x, g, wz, q_norm, kv_norm, wqn, wqr, wqs, wkn, wv, qg, qgs, kg, kgs, cos2, sin2)


def _attn_kernel(q_ref, k_ref, v_ref, o_ref, s_scr, p_scr, m_ref, acc_ref, *, sub):
    n_sub = k_ref.shape[2] // sub
    assert n_sub == 1 or n_sub % 2 == 0
    q = q_ref[0, 0]
    dv = o_ref.shape[2]

    def scores(j, slot):
        rows = pl.ds(pl.multiple_of(j * sub, sub), sub)
        s_scr[slot] = lax.dot_general(q, k_ref[0, 0, rows, :], _NT, preferred_element_type=F32)

    def weighted(j, slot):
        rows = pl.ds(pl.multiple_of(j * sub, sub), sub)
        return jnp.dot(p_scr[slot], v_ref[0, 0, rows, :], preferred_element_type=F32)

    def softmax(slot):
        s = s_scr[slot]
        m_old = m_ref[...]
        m_new = jnp.maximum(m_old, jnp.max(s, axis=-1, keepdims=True))
        alpha = jnp.exp2(m_old - m_new)
        m_ref[...] = m_new
        p_scr[slot] = jnp.exp2(s - jnp.concatenate([m_new] * (sub // LANES), axis=1)).astype(BF16)
        return jnp.concatenate([alpha] * (acc_ref.shape[1] // LANES), axis=1)

    def stage(j, slot, first=False, last=False):
        if not last:
            scores(j + 1, 1 - slot)
        pv = None if first else weighted(j - 1, 1 - slot)
        alpha = softmax(slot)
        if not first:
            acc_ref[...] = alpha * (acc_ref[...] + pv)

    m_ref[...] = jnp.full_like(m_ref, -jnp.inf)
    acc_ref[...] = jnp.zeros_like(acc_ref)
    scores(0, 0)
    if n_sub == 1:
        stage(0, 0, first=True, last=True)
    else:
        stage(0, 0, first=True)

        def two_stages(i, carry):
            stage(2 * i + 1, 1)
            stage(2 * i + 2, 0)
            return carry

        lax.fori_loop(0, (n_sub - 2) // 2, two_stages, 0)
        stage(n_sub - 1, 1, last=True)
    acc = acc_ref[...] + weighted(n_sub - 1, (n_sub - 1) % 2)
    o_ref[0] = (acc[:, :dv] / acc[:, dv:]).astype(o_ref.dtype)


def _attention(q, k, v, *, tq=512, sub=2048):
    b, h, t, dq = q.shape
    dv = v.shape[-1] // 2
    tq = min(tq, t)
    sub = min(sub, t // 2)
    return pl.pallas_call(
        functools.partial(_attn_kernel, sub=sub),
        out_shape=jax.ShapeDtypeStruct((b, t, h * dv), BF16),
        grid=(b, h, t // tq),
        in_specs=[
            pl.BlockSpec((1, 1, tq, dq), lambda i, j, qi: (i, j, qi, 0)),
            pl.BlockSpec((1, 1, t, dq), lambda i, j, qi: (i, j, 0, 0)),
            pl.BlockSpec((1, 1, t, 2 * dv), lambda i, j, qi: (i, j, 0, 0)),
        ],
        out_specs=pl.BlockSpec((1, tq, dv), lambda i, j, qi: (i, qi, j)),
        scratch_shapes=[
            pltpu.VMEM((2, tq, sub), F32),
            pltpu.VMEM((2, tq, sub), BF16),
            pltpu.VMEM((tq, LANES), F32),
            pltpu.VMEM((tq, 2 * dv), F32),
        ],
        compiler_params=_params("parallel", "parallel", "arbitrary"),
        name="mla_attention",
    )(q, k, v)


def _group_norm(x, g, b):
    mean = _head_sum(x) * (1.0 / RWKV_HEAD_DIM)
    xc = x - mean
    var = _head_sum(xc * xc) * (1.0 / RWKV_HEAD_DIM)
    return xc * lax.rsqrt(var + GN_EPS) * g + b


def _merge_kernel(x_ref, wf_ref, wb_ref, bonus_ref, g_ref, ob_ref, ga_ref, gb_ref,
                  lng_ref, lnb_ref, wa_ref, wbr_ref, wo_ref, o_ref):
    lng = lng_ref[...]
    lnb = lnb_ref[...]
    gn = _group_norm(wf_ref[...].astype(F32), lng, lnb) + _group_norm(wb_ref[...].astype(F32), lng, lnb)
    o_a = (gn + bonus_ref[...].astype(F32)) * g_ref[...].astype(F32)
    br_a = jnp.dot(o_a.astype(BF16), wa_ref[...], preferred_element_type=F32)
    br_b = jnp.dot(ob_ref[...], wbr_ref[...], preferred_element_type=F32)
    merged = ga_ref[...].astype(F32) * br_a + gb_ref[...].astype(F32) * br_b
    o_ref[...] = x_ref[...] + jnp.dot(merged.astype(BF16), wo_ref[...], preferred_element_type=F32)


def _merge(x, wkv_f, wkv_b, bonus, g, o_b, gates, ln_g, ln_b, w_a, w_b, w_o, *, tm=256):
    m, d = x.shape
    w = RWKV_WIDTH
    row_w = pl.BlockSpec((tm, w), lambda i: (i, 0))
    return pl.pallas_call(
        _merge_kernel,
        out_shape=jax.ShapeDtypeStruct((m, d), F32),
        grid=(m // tm,),
        in_specs=[
            pl.BlockSpec((tm, d), lambda i: (i, 0)),
            row_w, row_w, row_w, row_w, row_w,
            pl.BlockSpec((tm, d), lambda i: (i, 0)),
            pl.BlockSpec((tm, d), lambda i: (i, 1)),
            _const_spec((1, w)),
            _const_spec((1, w)),
            _const_spec(w_a.shape),
            _const_spec(w_b.shape),
            _const_spec(w_o.shape),
        ],
        out_specs=pl.BlockSpec((tm, d), lambda i: (i, 0)),
        compiler_params=_params("parallel"),
        name="merge",
    )(x, wkv_f, wkv_b, bonus, g, o_b, gates, gates, ln_g, ln_b, w_a, w_b, w_o)


def _swap_halves(x, axis=-1):
    lo, hi = jnp.split(x, 2, axis=axis)
    return jnp.concatenate([hi, lo], axis=axis)


def _prepare_weights(p):
    w = {}
    w['ffn1_w_in'] = p['ffn1_w_in'].astype(BF16)
    w['ffn1_w_out'] = p['ffn1_w_out'].astype(BF16)
    w['ffn2_w_in'] = p['ffn2_w_in'].astype(BF16)
    w['ffn2_w_out'] = p['ffn2_w_out'].astype(BF16)
    w_in = p['w_in']
    w['w_rwkv'] = w_in[:, :RWKV_IN].astype(BF16)
    mla = w_in[:, RWKV_IN:RWKV_IN + MLA_IN]
    w['w_mla'] = jnp.concatenate([mla, _swap_halves(mla[:, MLA_IN - QK_ROPE:])], axis=1).astype(BF16)
    w['w_gate'] = w_in[:, RWKV_IN + MLA_IN:].astype(BF16)

    def lora_rows(mat, slot):
        z = jnp.zeros((LORA_SLAB, RWKV_WIDTH), F32)
        return lax.dynamic_update_slice(z, mat, (slot * DECAY_LORA, 0)).astype(BF16)

    w['w2f'] = lora_rows(p['rwkv_w2'][0], 0)
    w['w2b'] = lora_rows(p['rwkv_w2'][1], 1)
    w['a2f'] = lora_rows(p['rwkv_a2'][0], 2)
    w['a2b'] = lora_rows(p['rwkv_a2'][1], 3)
    w['g2'] = p['rwkv_g2'].astype(BF16)

    w_uq = p['mla_w_uq'].reshape(Q_LORA, MLA_HEADS, QK_DIM)
    w['wqn'] = w_uq[:, :, :QK_NOPE].reshape(Q_LORA, MLA_HEADS * QK_NOPE).astype(BF16)
    wqr = w_uq[:, :, QK_NOPE:]
    w['wqr'] = wqr.reshape(Q_LORA, MLA_HEADS * QK_ROPE).astype(BF16)
    w['wqs'] = _swap_halves(wqr).reshape(Q_LORA, MLA_HEADS * QK_ROPE).astype(BF16)
    w_ukv = p['mla_w_ukv'].reshape(KV_LORA, MLA_HEADS, QK_NOPE + V_HEAD)
    w['wkn'] = w_ukv[:, :, :QK_NOPE].reshape(KV_LORA, MLA_HEADS * QK_NOPE).astype(BF16)
    w['wv'] = w_ukv[:, :, QK_NOPE:].reshape(KV_LORA, MLA_HEADS * V_HEAD).astype(BF16)
    w['qgs'] = _swap_halves(p['mla_q_gain'][QK_NOPE:])[None, :]
    w['kgs'] = _swap_halves(p['mla_k_gain'][QK_NOPE:])[None, :]
    w['w_a'] = p['w_branch_a'].astype(BF16)
    w['w_b'] = p['w_branch_b'].astype(BF16)
    w['w_o'] = p['w_out'].astype(BF16)
    return w


def _rope_tables(t):
    inv = ROPE_BASE ** (-jnp.arange(0, QK_ROPE, 2, dtype=F32) / QK_ROPE)
    ang = jnp.arange(t, dtype=F32)[:, None] * inv[None, :]
    cos, sin = jnp.cos(ang), jnp.sin(ang)
    return jnp.concatenate([cos, cos], axis=1), jnp.concatenate([-sin, sin], axis=1)


def _row(vec):
    return vec.reshape(1, -1)


def _encoder_layer(x, p, w):
    b, t, d = x.shape
    m = b * t
    x0 = x.reshape(m, d)
    x1 = _ffn(x0, _row(p['ffn1_norm']), w['ffn1_w_in'], w['ffn1_w_out'])
    mix_g = _row(p['mix_norm'])
    z_rwkv = _norm_matmul(x1, mix_g, w['w_rwkv'], tm=256, tn=RWKV_IN, name="in_proj_rwkv")
    gates = _norm_matmul(x1, mix_g, w['w_gate'], tm=1024, tn=1024, name="in_proj_gate", gate=True)

    (r, v, kk, g, bonus, lw_f, kd_f, as_f, lw_b, kd_b, as_b) = _rwkv_prep(
        z_rwkv.reshape(b, t, RWKV_IN), _row(p['rwkv_mu']), p['rwkv_w0'], p['rwkv_a0'],
        w['w2f'], w['w2b'], w['a2f'], w['a2b'], w['g2'],
        _row(p['rwkv_k_k']), _row(p['rwkv_k_a']), _row(p['rwkv_r_k']))
    wkv_f, wkv_b = _wkv_scan(r, v, kk, lw_f, kd_f, as_f, lw_b, kd_b, as_b)

    cos2, sin2 = _rope_tables(t)
    q, k, val = _mla_prep(
        x1.reshape(b, t, d), mix_g, w['w_mla'], _row(p['mla_q_norm']), _row(p['mla_kv_norm']),
        w['wqn'], w['wqr'], w['wqs'], w['wkn'], w['wv'],
        _row(p['mla_q_gain']), w['qgs'], _row(p['mla_k_gain']), w['kgs'], cos2, sin2)
    o_b = _attention(q, k, val)

    flat = lambda u: u.reshape(m, -1)
    x2 = _merge(x1, flat(wkv_f), flat(wkv_b), flat(bonus), flat(g), flat(o_b), gates,
                _row(p['rwkv_ln_g']), _row(p['rwkv_ln_b']), w['w_a'], w['w_b'], w['w_o'])
    y = _ffn(x2, _row(p['ffn2_norm']), w['ffn2_w_in'], w['ffn2_w_out'], _row(p['out_norm']))
    return y.reshape(b, t, d)


def kernel(x_prompt, x_sample, ffn1_norm, ffn1_w_in, ffn1_w_out, mix_norm, w_in, rwkv_mu, rwkv_w0, rwkv_w2, rwkv_a0, rwkv_a2, rwkv_g2, rwkv_k_k, rwkv_k_a, rwkv_r_k, rwkv_ln_g, rwkv_ln_b, mla_q_norm, mla_w_uq, mla_kv_norm, mla_w_ukv, mla_q_gain, mla_k_gain, w_branch_a, w_branch_b, w_out, ffn2_norm, ffn2_w_in, ffn2_w_out, out_norm):
    stacked = dict(
        ffn1_norm=ffn1_norm, ffn1_w_in=ffn1_w_in, ffn1_w_out=ffn1_w_out, mix_norm=mix_norm, w_in=w_in,
        rwkv_mu=rwkv_mu, rwkv_w0=rwkv_w0, rwkv_w2=rwkv_w2, rwkv_a0=rwkv_a0, rwkv_a2=rwkv_a2,
        rwkv_g2=rwkv_g2, rwkv_k_k=rwkv_k_k, rwkv_k_a=rwkv_k_a, rwkv_r_k=rwkv_r_k,
        rwkv_ln_g=rwkv_ln_g, rwkv_ln_b=rwkv_ln_b, mla_q_norm=mla_q_norm, mla_w_uq=mla_w_uq,
        mla_kv_norm=mla_kv_norm, mla_w_ukv=mla_w_ukv, mla_q_gain=mla_q_gain, mla_k_gain=mla_k_gain,
        w_branch_a=w_branch_a, w_branch_b=w_branch_b, w_out=w_out, ffn2_norm=ffn2_norm,
        ffn2_w_in=ffn2_w_in, ffn2_w_out=ffn2_w_out, out_norm=out_norm)
    y_prompt, y_sample = x_prompt, x_sample
    for layer in range(ffn1_norm.shape[0]):
        p = {name: arr[layer] for name, arr in stacked.items()}
        p['rwkv_r_k'] = p['rwkv_r_k'].reshape(-1)
        w = _prepare_weights(p)
        y_prompt = _encoder_layer(y_prompt, p, w)
        y_sample = _encoder_layer(y_sample, p, w)
    return (y_prompt, y_sample)
```

```python
import functools
import math

import jax
import jax.numpy as jnp
from jax import lax
from jax.experimental import pallas as pl
from jax.experimental.pallas import tpu as pltpu

F32 = jnp.float32
BF16 = jnp.bfloat16

D_MODEL = 2048
D_FF = 5632
NORM_EPS = 1e-6

RWKV_HEADS = 16
RWKV_HEAD_DIM = 64
RWKV_WIDTH = RWKV_HEADS * RWKV_HEAD_DIM
DECAY_LORA = 96
AAA_LORA = 96
GATE_LORA = 256
LORA_SLAB = 2 * DECAY_LORA + 2 * AAA_LORA
GN_EPS = 64e-5
RWKV_IN = 3 * RWKV_WIDTH + LORA_SLAB + GATE_LORA

MLA_HEADS = 8
Q_LORA = 512
KV_LORA = 512
QK_NOPE = 128
QK_ROPE = 64
QK_DIM = QK_NOPE + QK_ROPE
V_HEAD = 128
MLA_WIDTH = MLA_HEADS * V_HEAD
MLA_IN = Q_LORA + KV_LORA + QK_ROPE
ROPE_TILE = 128
MLA_COLS = Q_LORA + KV_LORA + 2 * ROPE_TILE
ROPE_BASE = 10000.0
GATE_IN = 2 * D_MODEL

LANES = 128
SUBLANES = 8
MXU_DIM = 256
HALO_ROWS = 2 * SUBLANES
VMEM_LIMIT_BYTES = 56 * 1024 * 1024

WKV_CHUNK = 64
WKV_DIAG = 16
WKV_PARTS = 1
WKV_PAIRS_PER_STEP = 8


def _params(*semantics):
    return pltpu.CompilerParams(dimension_semantics=semantics, vmem_limit_bytes=VMEM_LIMIT_BYTES)


def _const_spec(shape):
    zeros = (0,) * len(shape)
    return pl.BlockSpec(shape, lambda *_: zeros, pipeline_mode=pl.Buffered(1))


def _rms(x, g):
    ms = jnp.mean(x * x, axis=-1, keepdims=True)
    return x * lax.rsqrt(ms + NORM_EPS) * g


def _sigmoid(x):
    return 0.5 * jnp.tanh(0.5 * x) + 0.5


def _bf16_parts(x, n):
    if x.dtype == BF16:
        return [x]
    parts = []
    rem = x
    for i in range(n):
        p = rem.astype(BF16)
        parts.append(p)
        if i + 1 < n:
            rem = rem - p.astype(F32)
    return parts


_NN = (((1,), (0,)), ((), ()))
_NT = (((1,), (1,)), ((), ()))
_TN = (((0,), (0,)), ((), ()))


def _dotp(a, b, dims=_NN, pa=1, pb=1):
    a_parts = _bf16_parts(a, pa)
    b_parts = _bf16_parts(b, pb)
    order = max(len(a_parts), len(b_parts)) - 1
    acc = None
    for i, ap in enumerate(a_parts):
        for j, bp in enumerate(b_parts):
            if i + j > order:
                continue
            t = lax.dot_general(ap, bp, dims, preferred_element_type=F32)
            acc = t if acc is None else acc + t
    return acc


def _ffn_kernel(x_ref, g_ref, wg_ref, wu_ref, wo_ref, *rest, final_norm):
    if final_norm:
        gf_ref, o_ref, xn_ref = rest
    else:
        o_ref, xn_ref = rest
    j = pl.program_id(1)

    @pl.when(j == 0)
    def _():
        xn_ref[...] = _rms(x_ref[...], g_ref[...]).astype(BF16)
        o_ref[...] = jnp.zeros_like(o_ref)

    xn = xn_ref[...]
    gate = jnp.dot(xn, wg_ref[...], preferred_element_type=F32)
    up = jnp.dot(xn, wu_ref[...], preferred_element_type=F32)
    h = (gate * _sigmoid(gate) * up).astype(BF16)
    o_ref[...] += jnp.dot(h, wo_ref[...], preferred_element_type=F32)

    @pl.when(j == pl.num_programs(1) - 1)
    def _():
        y = x_ref[...] + 0.5 * o_ref[...]
        if final_norm:
            y = _rms(y, gf_ref[...])
        o_ref[...] = y


def _ffn(x, g, w_in, w_out, g_final=None, *, tm=1024, tf=512):
    m, d = x.shape
    n_ff = w_out.shape[0]
    nj = n_ff // tf
    in_specs = [
        pl.BlockSpec((tm, d), lambda i, j: (i, 0), pipeline_mode=pl.Buffered(1)),
        pl.BlockSpec((1, d), lambda i, j: (0, 0)),
        pl.BlockSpec((d, tf), lambda i, j: (0, j)),
        pl.BlockSpec((d, tf), lambda i, j: (0, j + nj)),
        pl.BlockSpec((tf, d), lambda i, j: (j, 0)),
    ]
    args = [x, g, w_in, w_in, w_out]
    if g_final is not None:
        in_specs.append(pl.BlockSpec((1, d), lambda i, j: (0, 0)))
        args.append(g_final)
    return pl.pallas_call(
        functools.partial(_ffn_kernel, final_norm=g_final is not None),
        out_shape=jax.ShapeDtypeStruct((m, d), F32),
        grid=(m // tm, nj),
        in_specs=in_specs,
        out_specs=pl.BlockSpec((tm, d), lambda i, j: (i, 0)),
        scratch_shapes=[pltpu.VMEM((tm, d), BF16)],
        compiler_params=_params("parallel", "arbitrary"),
        name="ffn",
    )(*args)


def _norm_matmul_kernel(x_ref, g_ref, w_ref, o_ref, xn_ref, *, gate):
    @pl.when(pl.program_id(1) == 0)
    def _():
        xn_ref[...] = _rms(x_ref[...], g_ref[...]).astype(BF16)

    z = jnp.dot(xn_ref[...], w_ref[...], preferred_element_type=F32)
    o_ref[...] = (_sigmoid(z) if gate else z).astype(o_ref.dtype)


def _norm_matmul(x, g, w, *, tm, tn, name, gate=False):
    m, d = x.shape
    n = w.shape[1]
    w_spec = _const_spec((d, n)) if tn == n else pl.BlockSpec((d, tn), lambda i, j: (0, j))
    return pl.pallas_call(
        functools.partial(_norm_matmul_kernel, gate=gate),
        out_shape=jax.ShapeDtypeStruct((m, n), BF16),
        grid=(m // tm, n // tn),
        in_specs=[
            pl.BlockSpec((tm, d), lambda i, j: (i, 0)),
            pl.BlockSpec((1, d), lambda i, j: (0, 0)),
            w_spec,
        ],
        out_specs=pl.BlockSpec((tm, tn), lambda i, j: (i, j)),
        scratch_shapes=[pltpu.VMEM((tm, d), BF16)],
        compiler_params=_params("parallel", "arbitrary"),
        name=name,
    )(x, g, w)


def _head_sum(x):
    wide = MXU_DIM
    r = lax.broadcasted_iota(jnp.int32, (wide, wide), 0) // RWKV_HEAD_DIM
    c = lax.broadcasted_iota(jnp.int32, (wide, wide), 1) // RWKV_HEAD_DIM
    ones = (r == c).astype(BF16)
    tiles = [
        _dotp(x[:, j * wide:(j + 1) * wide], ones, pa=1, pb=1)
        for j in range(x.shape[1] // wide)
    ]
    return jnp.concatenate(tiles, axis=1)


def _rwkv_prep_kernel(z_ref, zp_ref, zn_ref, mu_ref, w0_ref, a0_ref, w2f_ref, w2b_ref, a2f_ref,
                      a2b_ref, g2_ref, kk_ref, ka_ref, rk_ref,
                      r_out, v_out, kkn_out, g_out, bonus_out,
                      lwf_out, kdf_out, asf_out, lwb_out, kdb_out, asb_out):
    j = pl.program_id(1)
    nj = pl.num_programs(1)
    z = z_ref[0].astype(F32)
    tm = z.shape[0]
    row = lax.broadcasted_iota(jnp.int32, z.shape, 0)
    halo_prev = jnp.where(j > 0, zp_ref[0, HALO_ROWS - 1:HALO_ROWS, :].astype(F32), 0.0)
    halo_next = jnp.where(j < nj - 1, zn_ref[0, 0:1, :].astype(F32), 0.0)
    prev = jnp.where(row == 0, halo_prev, pltpu.roll(z, 1, axis=0))
    nxt = jnp.where(row == tm - 1, halo_next, pltpu.roll(z, tm - 1, axis=0))
    zs = z + mu_ref[...] * (0.5 * (prev + nxt) - z)

    w = RWKV_WIDTH
    r = zs[:, 0:w]
    k = zs[:, w:2 * w]
    v = zs[:, 2 * w:3 * w]
    lora = zs[:, 3 * w:3 * w + LORA_SLAB]
    gd = zs[:, 3 * w + LORA_SLAB:]

    g_out[0] = jnp.dot(_sigmoid(gd).astype(BF16), g2_ref[...], preferred_element_type=F32).astype(g_out.dtype)
    kk = k * kk_ref[...]
    kk = kk * lax.rsqrt(_head_sum(kk * kk) + 1e-12)
    r_out[0] = r.astype(r_out.dtype)
    v_out[0] = v.astype(v_out.dtype)
    kkn_out[0] = kk.astype(kkn_out.dtype)

    lora_t = jnp.tanh(lora).astype(BF16)
    lora_l = lora.astype(BF16)
    bonus = None
    for d, (w2_ref, a2_ref, lw_out, kd_out, as_out) in enumerate(
            ((w2f_ref, a2f_ref, lwf_out, kdf_out, asf_out), (w2b_ref, a2b_ref, lwb_out, kdb_out, asb_out))):
        w_pre = w0_ref[d:d + 1, :] + jnp.dot(lora_t, w2_ref[...], preferred_element_type=F32)
        lw_out[0] = -math.exp(-0.5) * _sigmoid(w_pre)
        a = _sigmoid(a0_ref[d:d + 1, :] + jnp.dot(lora_l, a2_ref[...], preferred_element_type=F32))
        k_d = k * (1.0 + (a - 1.0) * ka_ref[...])
        kd_out[0] = k_d.astype(kd_out.dtype)
        as_out[0] = a.astype(as_out.dtype)
        b_d = _head_sum(r * k_d * rk_ref[...]) * v
        bonus = b_d if bonus is None else bonus + b_d
    bonus_out[0] = bonus.astype(bonus_out.dtype)


def _rwkv_prep(z, mu, w0, a0, w2f, w2b, a2f, a2b, g2, k_k, k_a, r_k, *, tm=256):
    b, t, zc = z.shape
    w = RWKV_WIDTH
    nb = tm // HALO_ROWS
    last_halo = t // HALO_ROWS - 1
    narrow = jax.ShapeDtypeStruct((b, t, w), BF16)
    wide = jax.ShapeDtypeStruct((b, t, w), F32)
    row_spec = pl.BlockSpec((1, tm, w), lambda i, j: (i, j, 0))
    return pl.pallas_call(
        _rwkv_prep_kernel,
        out_shape=[narrow] * 5 + [wide, narrow, narrow] * 2,
        grid=(b, t // tm),
        in_specs=[
            pl.BlockSpec((1, tm, zc), lambda i, j: (i, j, 0)),
            pl.BlockSpec((1, HALO_ROWS, zc), lambda i, j: (i, jnp.maximum(j * nb - 1, 0), 0)),
            pl.BlockSpec((1, HALO_ROWS, zc), lambda i, j: (i, jnp.minimum((j + 1) * nb, last_halo), 0)),
            _const_spec((1, zc)),
            _const_spec((2, w)),
            _const_spec((2, w)),
            _const_spec((LORA_SLAB, w)),
            _const_spec((LORA_SLAB, w)),
            _const_spec((LORA_SLAB, w)),
            _const_spec((LORA_SLAB, w)),
            _const_spec((GATE_LORA, w)),
            _const_spec((1, w)),
            _const_spec((1, w)),
            _const_spec((1, w)),
        ],
        out_specs=[row_spec] * 11,
        compiler_params=_params("parallel", "parallel"),
        name="rwkv_prep",
    )(z, z, z, mu, w0, a0, w2f, w2b, a2f, a2b, g2, k_k, k_a, r_k)


def _wkv_chunk(r, lw, k, v, kk, asig, h_state, reverse):
    c = r.shape[0]
    n = 2 * c
    mm = functools.partial(_dotp, pa=WKV_PARTS, pb=WKV_PARTS)
    a = -kk
    b = kk * asig

    ti = lax.broadcasted_iota(jnp.int32, (c, c), 0)
    si = lax.broadcasted_iota(jnp.int32, (c, c), 1)
    tri = ((si >= ti) if reverse else (si <= ti)).astype(BF16)
    cum = _dotp(tri, lw, pa=1, pb=3)
    yield

    cum_x = cum - lw
    total = jnp.sum(lw, axis=0, keepdims=True)
    e_in = jnp.exp(cum)
    e_ex = jnp.exp(cum_x)
    e_neg = jnp.exp(-cum)
    e_rest = jnp.exp(total - cum)

    lane = lax.broadcasted_iota(jnp.int32, (c, LANES), 1)
    first = lane < RWKV_HEAD_DIM

    def stack(x):
        return jnp.concatenate([jnp.where(first, x, 0.0), jnp.where(first, 0.0, x)], axis=0)

    lhs = jnp.concatenate([stack(a * e_ex), stack(r * e_in)], axis=0)
    rhs = jnp.concatenate([stack(b * e_neg), stack(k * e_neg)], axis=0)
    aa = mm(lhs, rhs, _NT)
    x1 = mm(lhs, h_state, _NT)
    vs = stack(v)
    khv = mm(vs, stack(k * e_rest), _TN)
    yield

    row = lax.broadcasted_iota(jnp.int32, (n, n), 0)
    col = lax.broadcasted_iota(jnp.int32, (n, n), 1)
    t_in = row % c
    s_in = col % c
    strict = (s_in > t_in) if reverse else (s_in < t_in)
    incl = (s_in >= t_in) if reverse else (s_in <= t_in)
    n_ab = jnp.where(strict, aa[:n, :n], 0.0)
    a_ak = jnp.where(strict, aa[:n, n:], 0.0)
    a_rb = jnp.where(incl, aa[n:, :n], 0.0)
    a_rk = jnp.where(incl, aa[n:, n:], 0.0)

    eye = (row == col).astype(F32)
    diag = (row // WKV_DIAG) == (col // WKV_DIAG)
    n_d = jnp.where(diag, n_ab, 0.0)
    n_o = jnp.where(diag, 0.0, n_ab)
    x = eye + n_d
    pw = mm(n_d, n_d)
    av = mm(jnp.concatenate([a_ak, a_rk], axis=0), vs)
    yield
    for _ in range(int(math.log2(WKV_DIAG)) - 2):
        x = x + mm(x, pw)
        pw = mm(pw, pw)
        yield
    t_d = x + mm(x, pw)
    yield
    e1 = mm(t_d, n_o)
    yield
    y = eye + e1
    pw = e1
    for _ in range(int(math.log2(c // WKV_DIAG)) - 1):
        pw = mm(pw, pw)
        yield
        y = y + mm(y, pw)
        yield
    t_inv = mm(y, t_d)
    yield
    us = mm(t_inv, x1[:n] + av[:n])
    yield
    os_ = x1[n:] + mm(a_rb, us) + av[n:]
    out = os_[:c] + os_[c:]
    h_new = jnp.exp(total) * h_state + mm(us, stack(b * e_rest), _TN) + khv
    return out, h_new


def _run_lockstep(chains):
    results = [None] * len(chains)
    live = list(range(len(chains)))
    while live:
        still = []
        for i in live:
            try:
                next(chains[i])
                still.append(i)
            except StopIteration as stop:
                results[i] = stop.value
        live = still
    return results


def _wkv_kernel(rf_ref, vf_ref, kkf_ref, lwf_ref, kdf_ref, asf_ref,
                rb_ref, vb_ref, kkb_ref, lwb_ref, kdb_ref, asb_ref,
                of_ref, ob_ref, hf_ref, hb_ref):
    @pl.when(pl.program_id(2) == 0)
    def _():
        hf_ref[...] = jnp.zeros_like(hf_ref)
        hb_ref[...] = jnp.zeros_like(hb_ref)

    chains = []
    for g in range(hf_ref.shape[0]):
        ls = slice(g * LANES, (g + 1) * LANES)
        fwd = [ref[0, :, ls].astype(F32) for ref in (rf_ref, lwf_ref, kdf_ref, vf_ref, kkf_ref, asf_ref)]
        bwd = [ref[0, :, ls].astype(F32) for ref in (rb_ref, lwb_ref, kdb_ref, vb_ref, kkb_ref, asb_ref)]
        chains.append(_wkv_chunk(*fwd, hf_ref[g], reverse=False))
        chains.append(_wkv_chunk(*bwd, hb_ref[g], reverse=True))
    results = _run_lockstep(chains)
    for g in range(hf_ref.shape[0]):
        ls = slice(g * LANES, (g + 1) * LANES)
        (o_f, h_f), (o_b, h_b) = results[2 * g], results[2 * g + 1]
        of_ref[0, :, ls] = o_f.astype(of_ref.dtype)
        hf_ref[g] = h_f
        ob_ref[0, :, ls] = o_b.astype(ob_ref.dtype)
        hb_ref[g] = h_b


def _wkv_scan(r, v, kk, lw_f, kd_f, as_f, lw_b, kd_b, as_b):
    b, t, w = r.shape
    c = WKV_CHUNK
    nc = t // c
    pairs = min(WKV_PAIRS_PER_STEP, w // LANES)
    wb = pairs * LANES
    fwd = pl.BlockSpec((1, c, wb), lambda i, p, s: (i, s, p))
    bwd = pl.BlockSpec((1, c, wb), lambda i, p, s: (i, nc - 1 - s, p))
    out = jax.ShapeDtypeStruct((b, t, w), BF16)
    state = pltpu.VMEM((pairs, LANES, LANES), F32)
    return pl.pallas_call(
        _wkv_kernel,
        out_shape=[out, out],
        grid=(b, w // wb, nc),
        in_specs=[fwd] * 6 + [bwd] * 6,
        out_specs=[fwd, bwd],
        scratch_shapes=[state, state],
        compiler_params=_params("parallel", "parallel", "arbitrary"),
        name="wkv_scan",
    )(r, v, kk, lw_f, kd_f, as_f, r, v, kk, lw_b, kd_b, as_b)


def _mla_prep_kernel(x_ref, g_ref, wz_ref, qn_ref, kvn_ref, wqn_ref, wqr_ref, wqs_ref, wkn_ref, wv_ref,
                     qg_ref, qgr_ref, qgs_ref, kg_ref, kgr_ref, kgs_ref, cos_ref, sin_ref, q_out, k_out, v_out):
    xn = _rms(x_ref[0], g_ref[...]).astype(BF16)
    z = jnp.dot(xn, wz_ref[...], preferred_element_type=F32)
    cq = _rms(z[:, :Q_LORA], qn_ref[...]).astype(BF16)
    ckv = _rms(z[:, Q_LORA:Q_LORA + KV_LORA], kvn_ref[...]).astype(BF16)
    kr = z[:, MLA_COLS - 2 * ROPE_TILE:MLA_COLS - ROPE_TILE]
    krs = z[:, MLA_COLS - ROPE_TILE:]
    q_nope = jnp.dot(cq, wqn_ref[...], preferred_element_type=F32)
    q_rope = jnp.dot(cq, wqr_ref[...], preferred_element_type=F32)
    q_swap = jnp.dot(cq, wqs_ref[...], preferred_element_type=F32)
    k_nope = jnp.dot(ckv, wkn_ref[...], preferred_element_type=F32)
    val = jnp.dot(ckv, wv_ref[...], preferred_element_type=F32)
    cos = cos_ref[...]
    sin = sin_ref[...]
    scale = math.log2(math.e) / math.sqrt(QK_DIM)
    ones_v = jnp.ones((val.shape[0], V_HEAD), BF16)
    kr_ss = jnp.sum(kr * kr, axis=-1, keepdims=True)
    for h in range(MLA_HEADS):
        nope = slice(h * QK_NOPE, (h + 1) * QK_NOPE)
        rope = slice(h * ROPE_TILE, (h + 1) * ROPE_TILE)
        qn, qr, qs = q_nope[:, nope], q_rope[:, rope], q_swap[:, rope]
        ss = jnp.sum(qn * qn, axis=-1, keepdims=True) + jnp.sum(qr * qr, axis=-1, keepdims=True)
        rs = lax.rsqrt(ss / QK_DIM + NORM_EPS)
        q_out[0, h, :, :QK_NOPE] = (qn * rs * qg_ref[:, :QK_NOPE] * scale).astype(BF16)
        rot = qr * rs * qgr_ref[...] * cos + qs * rs * qgs_ref[...] * sin
        q_out[0, h, :, QK_NOPE:] = (rot[:, :QK_ROPE] * scale).astype(BF16)
        kn = k_nope[:, nope]
        ss = jnp.sum(kn * kn, axis=-1, keepdims=True) + kr_ss
        rs = lax.rsqrt(ss / QK_DIM + NORM_EPS)
        k_out[0, h, :, :QK_NOPE] = (kn * rs * kg_ref[:, :QK_NOPE]).astype(BF16)
        rot = kr * rs * kgr_ref[...] * cos + krs * rs * kgs_ref[...] * sin
        k_out[0, h, :, QK_NOPE:] = rot[:, :QK_ROPE].astype(BF16)
        v_out[0, h, :, :V_HEAD] = val[:, nope].astype(BF16)
        v_out[0, h, :, V_HEAD:] = ones_v


def _mla_prep(x, g, wz, q_norm, kv_norm, wqn, wqr, wqs, wkn, wv, qg, qgr, qgs, kg, kgr, kgs, cos2, sin2, *, tm=256):
    b, t, d = x.shape
    h = MLA_HEADS
    return pl.pallas_call(
        _mla_prep_kernel,
        out_shape=[
            jax.ShapeDtypeStruct((b, h, t, QK_DIM), BF16),
            jax.ShapeDtypeStruct((b, h, t, QK_DIM), BF16),
            jax.ShapeDtypeStruct((b, h, t, 2 * V_HEAD), BF16),
        ],
        grid=(b, t // tm),
        in_specs=[
            pl.BlockSpec((1, tm, d), lambda i, j: (i, j, 0)),
            _const_spec((1, d)),
            _const_spec(wz.shape),
            _const_spec((1, Q_LORA)),
            _const_spec((1, KV_LORA)),
            _const_spec(wqn.shape),
            _const_spec(wqr.shape),
            _const_spec(wqs.shape),
            _const_spec(wkn.shape),
            _const_spec(wv.shape),
            _const_spec((1, QK_DIM)),
            _const_spec((1, ROPE_TILE)),
            _const_spec((1, ROPE_TILE)),
            _const_spec((1, QK_DIM)),
            _const_spec((1, ROPE_TILE)),
            _const_spec((1, ROPE_TILE)),
            pl.BlockSpec((tm, ROPE_TILE), lambda i, j: (j, 0)),
            pl.BlockSpec((tm, ROPE_TILE), lambda i, j: (j, 0)),
        ],
        out_specs=[
            pl.BlockSpec((1, h, tm, QK_DIM), lambda i, j: (i, 0, j, 0)),
            pl.BlockSpec((1, h, tm, QK_DIM), lambda i, j: (i, 0, j, 0)),
            pl.BlockSpec((1, h, tm, 2 * V_HEAD), lambda i, j: (i, 0, j, 0)),
        ],
        compiler_params=_params("parallel", "parallel"),
        name="mla_prep",
    )(x, g, wz, q_norm, kv_norm, wqn, wqr, wqs, wkn, wv, qg, qgr, qgs, kg, kgr, kgs, cos2, sin2)


def _attn_kernel(q_ref, k_ref, v_ref, o_ref, s_scr, p_scr, m_ref, acc_ref, *, sub):
    n_sub = k_ref.shape[2] // sub
    assert n_sub == 1 or n_sub % 2 == 0
    q = q_ref[0, 0]
    dv = o_ref.shape[2]

    def scores(j, slot):
        rows = pl.ds(pl.multiple_of(j * sub, sub), sub)
        s_scr[slot] = lax.dot_general(q, k_ref[0, 0, rows, :], _NT, preferred_element_type=F32)

    def weighted(j, slot):
        rows = pl.ds(pl.multiple_of(j * sub, sub), sub)
        return jnp.dot(p_scr[slot], v_ref[0, 0, rows, :], preferred_element_type=F32)

    def softmax(slot):
        s = s_scr[slot]
        m_old = m_ref[...]
        m_new = jnp.maximum(m_old, jnp.max(s, axis=-1, keepdims=True))
        alpha = jnp.exp2(m_old - m_new)
        m_ref[...] = m_new
        p_scr[slot] = jnp.exp2(s - jnp.concatenate([m_new] * (sub // LANES), axis=1)).astype(BF16)
        return jnp.concatenate([alpha] * (acc_ref.shape[1] // LANES), axis=1)

    def stage(j, slot, first=False, last=False):
        if not last:
            scores(j + 1, 1 - slot)
        pv = None if first else weighted(j - 1, 1 - slot)
        alpha = softmax(slot)
        if not first:
            acc_ref[...] = alpha * (acc_ref[...] + pv)

    m_ref[...] = jnp.full_like(m_ref, -jnp.inf)
    acc_ref[...] = jnp.zeros_like(acc_ref)
    scores(0, 0)
    if n_sub == 1:
        stage(0, 0, first=True, last=True)
    else:
        stage(0, 0, first=True)

        def two_stages(i, carry):
            stage(2 * i + 1, 1)
            stage(2 * i + 2, 0)
            return carry

        lax.fori_loop(0, (n_sub - 2) // 2, two_stages, 0)
        stage(n_sub - 1, 1, last=True)
    acc = acc_ref[...] + weighted(n_sub - 1, (n_sub - 1) % 2)
    o_ref[0] = (acc[:, :dv] / acc[:, dv:]).astype(o_ref.dtype)


def _attention(q, k, v, *, tq=512, sub=2048):
    b, h, t, dq = q.shape
    dv = v.shape[-1] // 2
    tq = min(tq, t)
    sub = min(sub, t // 2)
    return pl.pallas_call(
        functools.partial(_attn_kernel, sub=sub),
        out_shape=jax.ShapeDtypeStruct((b, t, h * dv), BF16),
        grid=(b, h, t // tq),
        in_specs=[
            pl.BlockSpec((1, 1, tq, dq), lambda i, j, qi: (i, j, qi, 0)),
            pl.BlockSpec((1, 1, t, dq), lambda i, j, qi: (i, j, 0, 0)),
            pl.BlockSpec((1, 1, t, 2 * dv), lambda i, j, qi: (i, j, 0, 0)),
        ],
        out_specs=pl.BlockSpec((1, tq, dv), lambda i, j, qi: (i, qi, j)),
        scratch_shapes=[
            pltpu.VMEM((2, tq, sub), F32),
            pltpu.VMEM((2, tq, sub), BF16),
            pltpu.VMEM((tq, LANES), F32),
            pltpu.VMEM((tq, 2 * dv), F32),
        ],
        compiler_params=_params("parallel", "parallel", "arbitrary"),
        name="mla_attention",
    )(q, k, v)


def _group_norm(x, g, b):
    mean = _head_sum(x) * (1.0 / RWKV_HEAD_DIM)
    xc = x - mean
    var = _head_sum(xc * xc) * (1.0 / RWKV_HEAD_DIM)
    return xc * lax.rsqrt(var + GN_EPS) * g + b


def _merge_kernel(x_ref, wf_ref, wb_ref, bonus_ref, g_ref, ob_ref, ga_ref, gb_ref,
                  lng_ref, lnb_ref, wa_ref, wbr_ref, wo_ref, o_ref):
    lng = lng_ref[...]
    lnb = lnb_ref[...]
    gn = _group_norm(wf_ref[...].astype(F32), lng, lnb) + _group_norm(wb_ref[...].astype(F32), lng, lnb)
    o_a = (gn + bonus_ref[...].astype(F32)) * g_ref[...].astype(F32)
    br_a = jnp.dot(o_a.astype(BF16), wa_ref[...], preferred_element_type=F32)
    br_b = jnp.dot(ob_ref[...], wbr_ref[...], preferred_element_type=F32)
    merged = ga_ref[...].astype(F32) * br_a + gb_ref[...].astype(F32) * br_b
    o_ref[...] = x_ref[...] + jnp.dot(merged.astype(BF16), wo_ref[...], preferred_element_type=F32)


def _merge(x, wkv_f, wkv_b, bonus, g, o_b, gates, ln_g, ln_b, w_a, w_b, w_o, *, tm=256):
    m, d = x.shape
    w = RWKV_WIDTH
    row_w = pl.BlockSpec((tm, w), lambda i: (i, 0))
    return pl.pallas_call(
        _merge_kernel,
        out_shape=jax.ShapeDtypeStruct((m, d), F32),
        grid=(m // tm,),
        in_specs=[
            pl.BlockSpec((tm, d), lambda i: (i, 0)),
            row_w, row_w, row_w, row_w, row_w,
            pl.BlockSpec((tm, d), lambda i: (i, 0)),
            pl.BlockSpec((tm, d), lambda i: (i, 1)),
            _const_spec((1, w)),
            _const_spec((1, w)),
            _const_spec(w_a.shape),
            _const_spec(w_b.shape),
            _const_spec(w_o.shape),
        ],
        out_specs=pl.BlockSpec((tm, d), lambda i: (i, 0)),
        compiler_params=_params("parallel"),
        name="merge",
    )(x, wkv_f, wkv_b, bonus, g, o_b, gates, gates, ln_g, ln_b, w_a, w_b, w_o)


def _rope_tile(x):
    return jnp.pad(x, [(0, 0)] * (x.ndim - 1) + [(0, ROPE_TILE - x.shape[-1])])


def _swap_halves(x, axis=-1):
    lo, hi = jnp.split(x, 2, axis=axis)
    return jnp.concatenate([hi, lo], axis=axis)


def _prepare_weights(p):
    w = {}
    w['ffn1_w_in'] = p['ffn1_w_in'].astype(BF16)
    w['ffn1_w_out'] = p['ffn1_w_out'].astype(BF16)
    w['ffn2_w_in'] = p['ffn2_w_in'].astype(BF16)
    w['ffn2_w_out'] = p['ffn2_w_out'].astype(BF16)
    w_in = p['w_in']
    w['w_rwkv'] = w_in[:, :RWKV_IN].astype(BF16)
    mla = w_in[:, RWKV_IN:RWKV_IN + MLA_IN]
    w['w_mla'] = jnp.concatenate(
        [mla[:, :Q_LORA + KV_LORA], _rope_tile(mla[:, MLA_IN - QK_ROPE:]),
         _rope_tile(_swap_halves(mla[:, MLA_IN - QK_ROPE:]))], axis=1).astype(BF16)
    w['w_gate'] = w_in[:, RWKV_IN + MLA_IN:].astype(BF16)

    def lora_rows(mat, slot):
        z = jnp.zeros((LORA_SLAB, RWKV_WIDTH), F32)
        return lax.dynamic_update_slice(z, mat, (slot * DECAY_LORA, 0)).astype(BF16)

    w['w2f'] = lora_rows(p['rwkv_w2'][0], 0)
    w['w2b'] = lora_rows(p['rwkv_w2'][1], 1)
    w['a2f'] = lora_rows(p['rwkv_a2'][0], 2)
    w['a2b'] = lora_rows(p['rwkv_a2'][1], 3)
    w['g2'] = p['rwkv_g2'].astype(BF16)

    w_uq = p['mla_w_uq'].reshape(Q_LORA, MLA_HEADS, QK_DIM)
    w['wqn'] = w_uq[:, :, :QK_NOPE].reshape(Q_LORA, MLA_HEADS * QK_NOPE).astype(BF16)
    wqr = w_uq[:, :, QK_NOPE:]
    w['wqr'] = _rope_tile(wqr).reshape(Q_LORA, MLA_HEADS * ROPE_TILE).astype(BF16)
    w['wqs'] = _rope_tile(_swap_halves(wqr)).reshape(Q_LORA, MLA_HEADS * ROPE_TILE).astype(BF16)
    w_ukv = p['mla_w_ukv'].reshape(KV_LORA, MLA_HEADS, QK_NOPE + V_HEAD)
    w['wkn'] = w_ukv[:, :, :QK_NOPE].reshape(KV_LORA, MLA_HEADS * QK_NOPE).astype(BF16)
    w['wv'] = w_ukv[:, :, QK_NOPE:].reshape(KV_LORA, MLA_HEADS * V_HEAD).astype(BF16)
    w['qgr'] = _rope_tile(p['mla_q_gain'][QK_NOPE:])[None, :]
    w['qgs'] = _rope_tile(_swap_halves(p['mla_q_gain'][QK_NOPE:]))[None, :]
    w['kgr'] = _rope_tile(p['mla_k_gain'][QK_NOPE:])[None, :]
    w['kgs'] = _rope_tile(_swap_halves(p['mla_k_gain'][QK_NOPE:]))[None, :]
    w['w_a'] = p['w_branch_a'].astype(BF16)
    w['w_b'] = p['w_branch_b'].astype(BF16)
    w['w_o'] = p['w_out'].astype(BF16)
    return w


def _rope_tables(t):
    inv = ROPE_BASE ** (-jnp.arange(0, QK_ROPE, 2, dtype=F32) / QK_ROPE)
    ang = jnp.arange(t, dtype=F32)[:, None] * inv[None, :]
    cos, sin = jnp.cos(ang), jnp.sin(ang)
    return _rope_tile(jnp.concatenate([cos, cos], axis=1)), _rope_tile(jnp.concatenate([-sin, sin], axis=1))


def _row(vec):
    return vec.reshape(1, -1)


def _encoder_layer(x, p, w):
    b, t, d = x.shape
    m = b * t
    x0 = x.reshape(m, d)
    x1 = _ffn(x0, _row(p['ffn1_norm']), w['ffn1_w_in'], w['ffn1_w_out'])
    mix_g = _row(p['mix_norm'])
    z_rwkv = _norm_matmul(x1, mix_g, w['w_rwkv'], tm=256, tn=RWKV_IN, name="in_proj_rwkv")
    gates = _norm_matmul(x1, mix_g, w['w_gate'], tm=1024, tn=1024, name="in_proj_gate", gate=True)

    (r, v, kk, g, bonus, lw_f, kd_f, as_f, lw_b, kd_b, as_b) = _rwkv_prep(
        z_rwkv.reshape(b, t, RWKV_IN), _row(p['rwkv_mu']), p['rwkv_w0'], p['rwkv_a0'],
        w['w2f'], w['w2b'], w['a2f'], w['a2b'], w['g2'],
        _row(p['rwkv_k_k']), _row(p['rwkv_k_a']), _row(p['rwkv_r_k']))
    wkv_f, wkv_b = _wkv_scan(r, v, kk, lw_f, kd_f, as_f, lw_b, kd_b, as_b)

    cos2, sin2 = _rope_tables(t)
    q, k, val = _mla_prep(
        x1.reshape(b, t, d), mix_g, w['w_mla'], _row(p['mla_q_norm']), _row(p['mla_kv_norm']),
        w['wqn'], w['wqr'], w['wqs'], w['wkn'], w['wv'],
        _row(p['mla_q_gain']), w['qgr'], w['qgs'], _row(p['mla_k_gain']), w['kgr'], w['kgs'], cos2, sin2)
    o_b = _attention(q, k, val)

    flat = lambda u: u.reshape(m, -1)
    x2 = _merge(x1, flat(wkv_f), flat(wkv_b), flat(bonus), flat(g), flat(o_b), gates,
                _row(p['rwkv_ln_g']), _row(p['rwkv_ln_b']), w['w_a'], w['w_b'], w['w_o'])
    y = _ffn(x2, _row(p['ffn2_norm']), w['ffn2_w_in'], w['ffn2_w_out'], _row(p['out_norm']))
    return y.reshape(b, t, d)


def kernel(x_prompt, x_sample, ffn1_norm, ffn1_w_in, ffn1_w_out, mix_norm, w_in, rwkv_mu, rwkv_w0, rwkv_w2, rwkv_a0, rwkv_a2, rwkv_g2, rwkv_k_k, rwkv_k_a, rwkv_r_k, rwkv_ln_g, rwkv_ln_b, mla_q_norm, mla_w_uq, mla_kv_norm, mla_w_ukv, mla_q_gain, mla_k_gain, w_branch_a, w_branch_b, w_out, ffn2_norm, ffn2_w_in, ffn2_w_out, out_norm):
    stacked = dict(
        ffn1_norm=ffn1_norm, ffn1_w_in=ffn1_w_in, ffn1_w_out=ffn1_w_out, mix_norm=mix_norm, w_in=w_in,
        rwkv_mu=rwkv_mu, rwkv_w0=rwkv_w0, rwkv_w2=rwkv_w2, rwkv_a0=rwkv_a0, rwkv_a2=rwkv_a2,
        rwkv_g2=rwkv_g2, rwkv_k_k=rwkv_k_k, rwkv_k_a=rwkv_k_a, rwkv_r_k=rwkv_r_k,
        rwkv_ln_g=rwkv_ln_g, rwkv_ln_b=rwkv_ln_b, mla_q_norm=mla_q_norm, mla_w_uq=mla_w_uq,
        mla_kv_norm=mla_kv_norm, mla_w_ukv=mla_w_ukv, mla_q_gain=mla_q_gain, mla_k_gain=mla_k_gain,
        w_branch_a=w_branch_a, w_branch_b=w_branch_b, w_out=w_out, ffn2_norm=ffn2_norm,
        ffn2_w_in=ffn2_w_in, ffn2_w_out=ffn2_w_out, out_norm=out_norm)
    y_prompt, y_sample = x_prompt, x_sample
    for layer in range(ffn1_norm.shape[0]):
        p = {name: arr[layer] for name, arr in stacked.items()}
        p['rwkv_r_k'] = p['rwkv_r_k'].reshape(-1)
        w = _prepare_weights(p)
        y_prompt = _encoder_layer(y_prompt, p, w)
        y_sample = _encoder_layer(y_sample, p, w)
    return (y_prompt, y_sample)
```

```python
import functools
import math

import jax
import jax.numpy as jnp
from jax import lax
from jax.experimental import pallas as pl
from jax.experimental.pallas import tpu as pltpu

F32 = jnp.float32
BF16 = jnp.bfloat16

D_MODEL = 2048
D_FF = 5632
NORM_EPS = 1e-6

RWKV_HEADS = 16
RWKV_HEAD_DIM = 64
RWKV_WIDTH = RWKV_HEADS * RWKV_HEAD_DIM
DECAY_LORA = 96
AAA_LORA = 96
GATE_LORA = 256
LORA_SLAB = 2 * DECAY_LORA + 2 * AAA_LORA
GN_EPS = 64e-5
RWKV_IN = 3 * RWKV_WIDTH + LORA_SLAB + GATE_LORA

MLA_HEADS = 8
Q_LORA = 512
KV_LORA = 512
QK_NOPE = 128
QK_ROPE = 64
QK_DIM = QK_NOPE + QK_ROPE
V_HEAD = 128
MLA_WIDTH = MLA_HEADS * V_HEAD
MLA_IN = Q_LORA + KV_LORA + QK_ROPE
ROPE_TILE = 128
MLA_COLS = Q_LORA + KV_LORA + 2 * ROPE_TILE
ROPE_BASE = 10000.0
GATE_IN = 2 * D_MODEL

LANES = 128
SUBLANES = 8
MXU_DIM = 256
HALO_ROWS = 2 * SUBLANES
VMEM_LIMIT_BYTES = 56 * 1024 * 1024

WKV_CHUNK = 64
WKV_DIAG = 16
WKV_PARTS = 1
WKV_PAIRS_PER_STEP = 8


def _params(*semantics):
    return pltpu.CompilerParams(dimension_semantics=semantics, vmem_limit_bytes=VMEM_LIMIT_BYTES)


def _const_spec(shape):
    zeros = (0,) * len(shape)
    return pl.BlockSpec(shape, lambda *_: zeros, pipeline_mode=pl.Buffered(1))


def _rms(x, g):
    ms = jnp.mean(x * x, axis=-1, keepdims=True)
    return x * lax.rsqrt(ms + NORM_EPS) * g


def _sigmoid(x):
    return 0.5 * jnp.tanh(0.5 * x) + 0.5


def _bf16_parts(x, n):
    if x.dtype == BF16:
        return [x]
    parts = []
    rem = x
    for i in range(n):
        p = rem.astype(BF16)
        parts.append(p)
        if i + 1 < n:
            rem = rem - p.astype(F32)
    return parts


_NN = (((1,), (0,)), ((), ()))
_NT = (((1,), (1,)), ((), ()))
_TN = (((0,), (0,)), ((), ()))


def _dotp(a, b, dims=_NN, pa=1, pb=1):
    a_parts = _bf16_parts(a, pa)
    b_parts = _bf16_parts(b, pb)
    order = max(len(a_parts), len(b_parts)) - 1
    acc = None
    for i, ap in enumerate(a_parts):
        for j, bp in enumerate(b_parts):
            if i + j > order:
                continue
            t = lax.dot_general(ap, bp, dims, preferred_element_type=F32)
            acc = t if acc is None else acc + t
    return acc


def _ffn_kernel(x_ref, g_ref, wg_ref, wu_ref, wo_ref, *rest, final_norm):
    if final_norm:
        gf_ref, o_ref, xn_ref = rest
    else:
        o_ref, xn_ref = rest
    j = pl.program_id(1)

    @pl.when(j == 0)
    def _():
        xn_ref[...] = _rms(x_ref[...], g_ref[...]).astype(BF16)
        o_ref[...] = jnp.zeros_like(o_ref)

    xn = xn_ref[...]
    gate = jnp.dot(xn, wg_ref[...], preferred_element_type=F32)
    up = jnp.dot(xn, wu_ref[...], preferred_element_type=F32)
    h = (gate * _sigmoid(gate) * up).astype(BF16)
    o_ref[...] += jnp.dot(h, wo_ref[...], preferred_element_type=F32)

    @pl.when(j == pl.num_programs(1) - 1)
    def _():
        y = x_ref[...] + 0.5 * o_ref[...]
        if final_norm:
            y = _rms(y, gf_ref[...])
        o_ref[...] = y


def _ffn(x, g, w_in, w_out, g_final=None, *, tm=512, tf=512):
    m, d = x.shape
    n_ff = w_out.shape[0]
    nj = n_ff // tf
    in_specs = [
        pl.BlockSpec((tm, d), lambda i, j: (i, 0)),
        pl.BlockSpec((1, d), lambda i, j: (0, 0)),
        pl.BlockSpec((d, tf), lambda i, j: (0, j)),
        pl.BlockSpec((d, tf), lambda i, j: (0, j + nj)),
        pl.BlockSpec((tf, d), lambda i, j: (j, 0)),
    ]
    args = [x, g, w_in, w_in, w_out]
    if g_final is not None:
        in_specs.append(pl.BlockSpec((1, d), lambda i, j: (0, 0)))
        args.append(g_final)
    return pl.pallas_call(
        functools.partial(_ffn_kernel, final_norm=g_final is not None),
        out_shape=jax.ShapeDtypeStruct((m, d), F32),
        grid=(m // tm, nj),
        in_specs=in_specs,
        out_specs=pl.BlockSpec((tm, d), lambda i, j: (i, 0)),
        scratch_shapes=[pltpu.VMEM((tm, d), BF16)],
        compiler_params=_params("parallel", "arbitrary"),
        name="ffn",
    )(*args)


def _norm_matmul_kernel(x_ref, g_ref, w_ref, o_ref, xn_ref, *, gate):
    @pl.when(pl.program_id(1) == 0)
    def _():
        xn_ref[...] = _rms(x_ref[...], g_ref[...]).astype(BF16)

    z = jnp.dot(xn_ref[...], w_ref[...], preferred_element_type=F32)
    o_ref[...] = (_sigmoid(z) if gate else z).astype(o_ref.dtype)


def _norm_matmul(x, g, w, *, tm, tn, name, gate=False):
    m, d = x.shape
    n = w.shape[1]
    w_spec = _const_spec((d, n)) if tn == n else pl.BlockSpec((d, tn), lambda i, j: (0, j))
    return pl.pallas_call(
        functools.partial(_norm_matmul_kernel, gate=gate),
        out_shape=jax.ShapeDtypeStruct((m, n), BF16),
        grid=(m // tm, n // tn),
        in_specs=[
            pl.BlockSpec((tm, d), lambda i, j: (i, 0)),
            pl.BlockSpec((1, d), lambda i, j: (0, 0)),
            w_spec,
        ],
        out_specs=pl.BlockSpec((tm, tn), lambda i, j: (i, j)),
        scratch_shapes=[pltpu.VMEM((tm, d), BF16)],
        compiler_params=_params("parallel", "arbitrary"),
        name=name,
    )(x, g, w)


def _head_sum(x):
    wide = MXU_DIM
    r = lax.broadcasted_iota(jnp.int32, (wide, wide), 0) // RWKV_HEAD_DIM
    c = lax.broadcasted_iota(jnp.int32, (wide, wide), 1) // RWKV_HEAD_DIM
    ones = (r == c).astype(BF16)
    tiles = [
        _dotp(x[:, j * wide:(j + 1) * wide], ones, pa=1, pb=1)
        for j in range(x.shape[1] // wide)
    ]
    return jnp.concatenate(tiles, axis=1)


def _rwkv_prep_kernel(z_ref, zp_ref, zn_ref, mu_ref, w0_ref, a0_ref, w2f_ref, w2b_ref, a2f_ref,
                      a2b_ref, g2_ref, kk_ref, ka_ref, rk_ref,
                      r_out, v_out, kkn_out, g_out, bonus_out,
                      lwf_out, kdf_out, asf_out, lwb_out, kdb_out, asb_out):
    j = pl.program_id(1)
    nj = pl.num_programs(1)
    z = z_ref[0].astype(F32)
    tm = z.shape[0]
    row = lax.broadcasted_iota(jnp.int32, z.shape, 0)
    halo_prev = jnp.where(j > 0, zp_ref[0, HALO_ROWS - 1:HALO_ROWS, :].astype(F32), 0.0)
    halo_next = jnp.where(j < nj - 1, zn_ref[0, 0:1, :].astype(F32), 0.0)
    prev = jnp.where(row == 0, halo_prev, pltpu.roll(z, 1, axis=0))
    nxt = jnp.where(row == tm - 1, halo_next, pltpu.roll(z, tm - 1, axis=0))
    zs = z + mu_ref[...] * (0.5 * (prev + nxt) - z)

    w = RWKV_WIDTH
    r = zs[:, 0:w]
    k = zs[:, w:2 * w]
    v = zs[:, 2 * w:3 * w]
    lora = zs[:, 3 * w:3 * w + LORA_SLAB]
    gd = zs[:, 3 * w + LORA_SLAB:]

    g_out[0] = jnp.dot(_sigmoid(gd).astype(BF16), g2_ref[...], preferred_element_type=F32).astype(g_out.dtype)
    kk = k * kk_ref[...]
    kk = kk * lax.rsqrt(_head_sum(kk * kk) + 1e-12)
    r_out[0] = r.astype(r_out.dtype)
    v_out[0] = v.astype(v_out.dtype)
    kkn_out[0] = kk.astype(kkn_out.dtype)

    lora_t = jnp.tanh(lora).astype(BF16)
    lora_l = lora.astype(BF16)
    bonus = None
    for d, (w2_ref, a2_ref, lw_out, kd_out, as_out) in enumerate(
            ((w2f_ref, a2f_ref, lwf_out, kdf_out, asf_out), (w2b_ref, a2b_ref, lwb_out, kdb_out, asb_out))):
        w_pre = w0_ref[d:d + 1, :] + jnp.dot(lora_t, w2_ref[...], preferred_element_type=F32)
        lw_out[0] = -math.exp(-0.5) * _sigmoid(w_pre)
        a = _sigmoid(a0_ref[d:d + 1, :] + jnp.dot(lora_l, a2_ref[...], preferred_element_type=F32))
        k_d = k * (1.0 + (a - 1.0) * ka_ref[...])
        kd_out[0] = k_d.astype(kd_out.dtype)
        as_out[0] = a.astype(as_out.dtype)
        b_d = _head_sum(r * k_d * rk_ref[...]) * v
        bonus = b_d if bonus is None else bonus + b_d
    bonus_out[0] = bonus.astype(bonus_out.dtype)


def _rwkv_prep(z, mu, w0, a0, w2f, w2b, a2f, a2b, g2, k_k, k_a, r_k, *, tm=256):
    b, t, zc = z.shape
    w = RWKV_WIDTH
    nb = tm // HALO_ROWS
    last_halo = t // HALO_ROWS - 1
    narrow = jax.ShapeDtypeStruct((b, t, w), BF16)
    wide = jax.ShapeDtypeStruct((b, t, w), F32)
    row_spec = pl.BlockSpec((1, tm, w), lambda i, j: (i, j, 0))
    return pl.pallas_call(
        _rwkv_prep_kernel,
        out_shape=[narrow] * 5 + [wide, narrow, narrow] * 2,
        grid=(b, t // tm),
        in_specs=[
            pl.BlockSpec((1, tm, zc), lambda i, j: (i, j, 0)),
            pl.BlockSpec((1, HALO_ROWS, zc), lambda i, j: (i, jnp.maximum(j * nb - 1, 0), 0)),
            pl.BlockSpec((1, HALO_ROWS, zc), lambda i, j: (i, jnp.minimum((j + 1) * nb, last_halo), 0)),
            _const_spec((1, zc)),
            _const_spec((2, w)),
            _const_spec((2, w)),
            _const_spec((LORA_SLAB, w)),
            _const_spec((LORA_SLAB, w)),
            _const_spec((LORA_SLAB, w)),
            _const_spec((LORA_SLAB, w)),
            _const_spec((GATE_LORA, w)),
            _const_spec((1, w)),
            _const_spec((1, w)),
            _const_spec((1, w)),
        ],
        out_specs=[row_spec] * 11,
        compiler_params=_params("parallel", "parallel"),
        name="rwkv_prep",
    )(z, z, z, mu, w0, a0, w2f, w2b, a2f, a2b, g2, k_k, k_a, r_k)


def _wkv_chunk(r, lw, k, v, kk, asig, h_state, reverse):
    c = r.shape[0]
    n = 2 * c
    mm = functools.partial(_dotp, pa=WKV_PARTS, pb=WKV_PARTS)
    a = -kk
    b = kk * asig

    ti = lax.broadcasted_iota(jnp.int32, (c, c), 0)
    si = lax.broadcasted_iota(jnp.int32, (c, c), 1)
    tri = ((si >= ti) if reverse else (si <= ti)).astype(BF16)
    cum = _dotp(tri, lw, pa=1, pb=3)
    yield

    cum_x = cum - lw
    total = jnp.sum(lw, axis=0, keepdims=True)
    e_in = jnp.exp(cum)
    e_ex = jnp.exp(cum_x)
    e_neg = jnp.exp(-cum)
    e_rest = jnp.exp(total - cum)

    lane = lax.broadcasted_iota(jnp.int32, (c, LANES), 1)
    first = lane < RWKV_HEAD_DIM

    def stack(x):
        return jnp.concatenate([jnp.where(first, x, 0.0), jnp.where(first, 0.0, x)], axis=0)

    lhs = jnp.concatenate([stack(a * e_ex), stack(r * e_in)], axis=0)
    rhs = jnp.concatenate([stack(b * e_neg), stack(k * e_neg)], axis=0)
    aa = mm(lhs, rhs, _NT)
    x1 = mm(lhs, h_state, _NT)
    vs = stack(v)
    khv = mm(vs, stack(k * e_rest), _TN)
    yield

    row = lax.broadcasted_iota(jnp.int32, (n, n), 0)
    col = lax.broadcasted_iota(jnp.int32, (n, n), 1)
    t_in = row % c
    s_in = col % c
    strict = (s_in > t_in) if reverse else (s_in < t_in)
    incl = (s_in >= t_in) if reverse else (s_in <= t_in)
    n_ab = jnp.where(strict, aa[:n, :n], 0.0)
    a_ak = jnp.where(strict, aa[:n, n:], 0.0)
    a_rb = jnp.where(incl, aa[n:, :n], 0.0)
    a_rk = jnp.where(incl, aa[n:, n:], 0.0)

    eye = (row == col).astype(F32)
    diag = (row // WKV_DIAG) == (col // WKV_DIAG)
    n_d = jnp.where(diag, n_ab, 0.0)
    n_o = jnp.where(diag, 0.0, n_ab)
    x = eye + n_d
    pw = mm(n_d, n_d)
    av = mm(jnp.concatenate([a_ak, a_rk], axis=0), vs)
    yield
    for _ in range(int(math.log2(WKV_DIAG)) - 2):
        x = x + mm(x, pw)
        pw = mm(pw, pw)
        yield
    t_d = x + mm(x, pw)
    yield
    e1 = mm(t_d, n_o)
    yield
    y = eye + e1
    pw = e1
    for _ in range(int(math.log2(c // WKV_DIAG)) - 1):
        pw = mm(pw, pw)
        yield
        y = y + mm(y, pw)
        yield
    t_inv = mm(y, t_d)
    yield
    us = mm(t_inv, x1[:n] + av[:n])
    yield
    os_ = x1[n:] + mm(a_rb, us) + av[n:]
    out = os_[:c] + os_[c:]
    h_new = jnp.exp(total) * h_state + mm(us, stack(b * e_rest), _TN) + khv
    return out, h_new


def _run_lockstep(chains):
    results = [None] * len(chains)
    live = list(range(len(chains)))
    while live:
        still = []
        for i in live:
            try:
                next(chains[i])
                still.append(i)
            except StopIteration as stop:
                results[i] = stop.value
        live = still
    return results


def _wkv_kernel(rf_ref, vf_ref, kkf_ref, lwf_ref, kdf_ref, asf_ref,
                rb_ref, vb_ref, kkb_ref, lwb_ref, kdb_ref, asb_ref,
                of_ref, ob_ref, hf_ref, hb_ref):
    @pl.when(pl.program_id(2) == 0)
    def _():
        hf_ref[...] = jnp.zeros_like(hf_ref)
        hb_ref[...] = jnp.zeros_like(hb_ref)

    chains = []
    for g in range(hf_ref.shape[0]):
        ls = slice(g * LANES, (g + 1) * LANES)
        fwd = [ref[0, :, ls].astype(F32) for ref in (rf_ref, lwf_ref, kdf_ref, vf_ref, kkf_ref, asf_ref)]
        bwd = [ref[0, :, ls].astype(F32) for ref in (rb_ref, lwb_ref, kdb_ref, vb_ref, kkb_ref, asb_ref)]
        chains.append(_wkv_chunk(*fwd, hf_ref[g], reverse=False))
        chains.append(_wkv_chunk(*bwd, hb_ref[g], reverse=True))
    results = _run_lockstep(chains)
    for g in range(hf_ref.shape[0]):
        ls = slice(g * LANES, (g + 1) * LANES)
        (o_f, h_f), (o_b, h_b) = results[2 * g], results[2 * g + 1]
        of_ref[0, :, ls] = o_f.astype(of_ref.dtype)
        hf_ref[g] = h_f
        ob_ref[0, :, ls] = o_b.astype(ob_ref.dtype)
        hb_ref[g] = h_b


def _wkv_scan(r, v, kk, lw_f, kd_f, as_f, lw_b, kd_b, as_b):
    b, t, w = r.shape
    c = WKV_CHUNK
    nc = t // c
    pairs = min(WKV_PAIRS_PER_STEP, w // LANES)
    wb = pairs * LANES
    fwd = pl.BlockSpec((1, c, wb), lambda i, p, s: (i, s, p))
    bwd = pl.BlockSpec((1, c, wb), lambda i, p, s: (i, nc - 1 - s, p))
    out = jax.ShapeDtypeStruct((b, t, w), BF16)
    state = pltpu.VMEM((pairs, LANES, LANES), F32)
    return pl.pallas_call(
        _wkv_kernel,
        out_shape=[out, out],
        grid=(b, w // wb, nc),
        in_specs=[fwd] * 6 + [bwd] * 6,
        out_specs=[fwd, bwd],
        scratch_shapes=[state, state],
        compiler_params=_params("parallel", "parallel", "arbitrary"),
        name="wkv_scan",
    )(r, v, kk, lw_f, kd_f, as_f, r, v, kk, lw_b, kd_b, as_b)


def _mla_prep_kernel(x_ref, g_ref, wz_ref, qn_ref, kvn_ref, wqn_ref, wqr_ref, wqs_ref, wkn_ref, wv_ref,
                     qg_ref, qgr_ref, qgs_ref, kg_ref, kgr_ref, kgs_ref, cos_ref, sin_ref, q_out, k_out, v_out):
    xn = _rms(x_ref[0], g_ref[...]).astype(BF16)
    z = jnp.dot(xn, wz_ref[...], preferred_element_type=F32)
    cq = _rms(z[:, :Q_LORA], qn_ref[...]).astype(BF16)
    ckv = _rms(z[:, Q_LORA:Q_LORA + KV_LORA], kvn_ref[...]).astype(BF16)
    kr = z[:, MLA_COLS - 2 * ROPE_TILE:MLA_COLS - ROPE_TILE]
    krs = z[:, MLA_COLS - ROPE_TILE:]
    q_nope = jnp.dot(cq, wqn_ref[...], preferred_element_type=F32)
    q_rope = jnp.dot(cq, wqr_ref[...], preferred_element_type=F32)
    q_swap = jnp.dot(cq, wqs_ref[...], preferred_element_type=F32)
    k_nope = jnp.dot(ckv, wkn_ref[...], preferred_element_type=F32)
    val = jnp.dot(ckv, wv_ref[...], preferred_element_type=F32)
    cos = cos_ref[...]
    sin = sin_ref[...]
    scale = math.log2(math.e) / math.sqrt(QK_DIM)
    ones_v = jnp.ones((val.shape[0], V_HEAD), BF16)
    kr_ss = jnp.sum(kr * kr, axis=-1, keepdims=True)
    for h in range(MLA_HEADS):
        nope = slice(h * QK_NOPE, (h + 1) * QK_NOPE)
        rope = slice(h * ROPE_TILE, (h + 1) * ROPE_TILE)
        qn, qr, qs = q_nope[:, nope], q_rope[:, rope], q_swap[:, rope]
        ss = jnp.sum(qn * qn, axis=-1, keepdims=True) + jnp.sum(qr * qr, axis=-1, keepdims=True)
        rs = lax.rsqrt(ss / QK_DIM + NORM_EPS)
        q_out[0, h, :, :QK_NOPE] = (qn * rs * qg_ref[:, :QK_NOPE] * scale).astype(BF16)
        rot = qr * rs * qgr_ref[...] * cos + qs * rs * qgs_ref[...] * sin
        q_out[0, h, :, QK_NOPE:] = (rot[:, :QK_ROPE] * scale).astype(BF16)
        kn = k_nope[:, nope]
        ss = jnp.sum(kn * kn, axis=-1, keepdims=True) + kr_ss
        rs = lax.rsqrt(ss / QK_DIM + NORM_EPS)
        k_out[0, h, :, :QK_NOPE] = (kn * rs * kg_ref[:, :QK_NOPE]).astype(BF16)
        rot = kr * rs * kgr_ref[...] * cos + krs * rs * kgs_ref[...] * sin
        k_out[0, h, :, QK_NOPE:] = rot[:, :QK_ROPE].astype(BF16)
        v_out[0, h, :, :V_HEAD] = val[:, nope].astype(BF16)
        v_out[0, h, :, V_HEAD:] = ones_v


def _mla_prep(x, g, wz, q_norm, kv_norm, wqn, wqr, wqs, wkn, wv, qg, qgr, qgs, kg, kgr, kgs, cos2, sin2, *, tm=256):
    b, t, d = x.shape
    h = MLA_HEADS
    return pl.pallas_call(
        _mla_prep_kernel,
        out_shape=[
            jax.ShapeDtypeStruct((b, h, t, QK_DIM), BF16),
            jax.ShapeDtypeStruct((b, h, t, QK_DIM), BF16),
            jax.ShapeDtypeStruct((b, h, t, 2 * V_HEAD), BF16),
        ],
        grid=(b, t // tm),
        in_specs=[
            pl.BlockSpec((1, tm, d), lambda i, j: (i, j, 0)),
            _const_spec((1, d)),
            _const_spec(wz.shape),
            _const_spec((1, Q_LORA)),
            _const_spec((1, KV_LORA)),
            _const_spec(wqn.shape),
            _const_spec(wqr.shape),
            _const_spec(wqs.shape),
            _const_spec(wkn.shape),
            _const_spec(wv.shape),
            _const_spec((1, QK_DIM)),
            _const_spec((1, ROPE_TILE)),
            _const_spec((1, ROPE_TILE)),
            _const_spec((1, QK_DIM)),
            _const_spec((1, ROPE_TILE)),
            _const_spec((1, ROPE_TILE)),
            pl.BlockSpec((tm, ROPE_TILE), lambda i, j: (j, 0)),
            pl.BlockSpec((tm, ROPE_TILE), lambda i, j: (j, 0)),
        ],
        out_specs=[
            pl.BlockSpec((1, h, tm, QK_DIM), lambda i, j: (i, 0, j, 0)),
            pl.BlockSpec((1, h, tm, QK_DIM), lambda i, j: (i, 0, j, 0)),
            pl.BlockSpec((1, h, tm, 2 * V_HEAD), lambda i, j: (i, 0, j, 0)),
        ],
        compiler_params=_params("parallel", "parallel"),
        name="mla_prep",
    )(x, g, wz, q_norm, kv_norm, wqn, wqr, wqs, wkn, wv, qg, qgr, qgs, kg, kgr, kgs, cos2, sin2)


def _attn_kernel(q_ref, k_ref, v_ref, o_ref, s_scr, p_scr, rmax_scr, m_ref, acc_ref, *, sub):
    n_sub = k_ref.shape[2] // sub
    assert n_sub == 1 or n_sub % 2 == 0
    q = q_ref[0, 0]
    dv = o_ref.shape[2]

    def scores(j, slot):
        rows = pl.ds(pl.multiple_of(j * sub, sub), sub)
        s = lax.dot_general(q, k_ref[0, 0, rows, :], _NT, preferred_element_type=F32)
        s_scr[slot] = s
        rmax_scr[slot] = jnp.broadcast_to(jnp.max(s, axis=-1, keepdims=True), rmax_scr.shape[1:])

    def weighted(j, slot):
        rows = pl.ds(pl.multiple_of(j * sub, sub), sub)
        return jnp.dot(p_scr[slot], v_ref[0, 0, rows, :], preferred_element_type=F32)

    def softmax(slot):
        m_old = m_ref[...]
        m_new = jnp.maximum(m_old, rmax_scr[slot])
        alpha = jnp.exp2(m_old - m_new)
        m_ref[...] = m_new
        p_scr[slot] = jnp.exp2(s_scr[slot] - jnp.concatenate([m_new] * (sub // LANES), axis=1)).astype(BF16)
        return jnp.concatenate([alpha] * (acc_ref.shape[1] // LANES), axis=1)

    def stage(j, slot, first=False, last=False):
        if not last:
            scores(j + 1, 1 - slot)
        pv = None if first else weighted(j - 1, 1 - slot)
        alpha = softmax(slot)
        if not first:
            acc_ref[...] = alpha * (acc_ref[...] + pv)

    m_ref[...] = jnp.full_like(m_ref, -jnp.inf)
    acc_ref[...] = jnp.zeros_like(acc_ref)
    scores(0, 0)
    if n_sub == 1:
        stage(0, 0, first=True, last=True)
    else:
        stage(0, 0, first=True)

        for j in range(1, n_sub - 1):
            stage(j, j % 2)
        stage(n_sub - 1, 1, last=True)
    acc = acc_ref[...] + weighted(n_sub - 1, (n_sub - 1) % 2)
    o_ref[0] = (acc[:, :dv] / acc[:, dv:]).astype(o_ref.dtype)


def _attention(q, k, v, *, tq=512, sub=2048):
    b, h, t, dq = q.shape
    dv = v.shape[-1] // 2
    tq = min(tq, t)
    sub = min(sub, t // 2)
    return pl.pallas_call(
        functools.partial(_attn_kernel, sub=sub),
        out_shape=jax.ShapeDtypeStruct((b, t, h * dv), BF16),
        grid=(b, h, t // tq),
        in_specs=[
            pl.BlockSpec((1, 1, tq, dq), lambda i, j, qi: (i, j, qi, 0)),
            pl.BlockSpec((1, 1, t, dq), lambda i, j, qi: (i, j, 0, 0)),
            pl.BlockSpec((1, 1, t, 2 * dv), lambda i, j, qi: (i, j, 0, 0)),
        ],
        out_specs=pl.BlockSpec((1, tq, dv), lambda i, j, qi: (i, qi, j)),
        scratch_shapes=[
            pltpu.VMEM((2, tq, sub), F32),
            pltpu.VMEM((2, tq, sub), BF16),
            pltpu.VMEM((2, tq, LANES), F32),
            pltpu.VMEM((tq, LANES), F32),
            pltpu.VMEM((tq, 2 * dv), F32),
        ],
        compiler_params=_params("parallel", "parallel", "arbitrary"),
        name="mla_attention",
    )(q, k, v)


def _group_norm(x, g, b):
    mean = _head_sum(x) * (1.0 / RWKV_HEAD_DIM)
    xc = x - mean
    var = _head_sum(xc * xc) * (1.0 / RWKV_HEAD_DIM)
    return xc * lax.rsqrt(var + GN_EPS) * g + b


def _merge_kernel(x_ref, wf_ref, wb_ref, bonus_ref, g_ref, ob_ref, ga_ref, gb_ref,
                  lng_ref, lnb_ref, wa_ref, wbr_ref, wo_ref, o_ref):
    lng = lng_ref[...]
    lnb = lnb_ref[...]
    gn = _group_norm(wf_ref[...].astype(F32), lng, lnb) + _group_norm(wb_ref[...].astype(F32), lng, lnb)
    o_a = (gn + bonus_ref[...].astype(F32)) * g_ref[...].astype(F32)
    br_a = jnp.dot(o_a.astype(BF16), wa_ref[...], preferred_element_type=F32)
    br_b = jnp.dot(ob_ref[...], wbr_ref[...], preferred_element_type=F32)
    merged = ga_ref[...].astype(F32) * br_a + gb_ref[...].astype(F32) * br_b
    o_ref[...] = x_ref[...] + jnp.dot(merged.astype(BF16), wo_ref[...], preferred_element_type=F32)


def _merge(x, wkv_f, wkv_b, bonus, g, o_b, gates, ln_g, ln_b, w_a, w_b, w_o, *, tm=256):
    m, d = x.shape
    w = RWKV_WIDTH
    row_w = pl.BlockSpec((tm, w), lambda i: (i, 0))
    return pl.pallas_call(
        _merge_kernel,
        out_shape=jax.ShapeDtypeStruct((m, d), F32),
        grid=(m // tm,),
        in_specs=[
            pl.BlockSpec((tm, d), lambda i: (i, 0)),
            row_w, row_w, row_w, row_w, row_w,
            pl.BlockSpec((tm, d), lambda i: (i, 0)),
            pl.BlockSpec((tm, d), lambda i: (i, 1)),
            _const_spec((1, w)),
            _const_spec((1, w)),
            _const_spec(w_a.shape),
            _const_spec(w_b.shape),
            _const_spec(w_o.shape),
        ],
        out_specs=pl.BlockSpec((tm, d), lambda i: (i, 0)),
        compiler_params=_params("parallel"),
        name="merge",
    )(x, wkv_f, wkv_b, bonus, g, o_b, gates, gates, ln_g, ln_b, w_a, w_b, w_o)


def _rope_tile(x):
    return jnp.pad(x, [(0, 0)] * (x.ndim - 1) + [(0, ROPE_TILE - x.shape[-1])])


def _swap_halves(x, axis=-1):
    lo, hi = jnp.split(x, 2, axis=axis)
    return jnp.concatenate([hi, lo], axis=axis)


def _prepare_weights(p):
    w = {}
    w['ffn1_w_in'] = p['ffn1_w_in'].astype(BF16)
    w['ffn1_w_out'] = p['ffn1_w_out'].astype(BF16)
    w['ffn2_w_in'] = p['ffn2_w_in'].astype(BF16)
    w['ffn2_w_out'] = p['ffn2_w_out'].astype(BF16)
    w_in = p['w_in']
    w['w_rwkv'] = w_in[:, :RWKV_IN].astype(BF16)
    mla = w_in[:, RWKV_IN:RWKV_IN + MLA_IN]
    w['w_mla'] = jnp.concatenate(
        [mla[:, :Q_LORA + KV_LORA], _rope_tile(mla[:, MLA_IN - QK_ROPE:]),
         _rope_tile(_swap_halves(mla[:, MLA_IN - QK_ROPE:]))], axis=1).astype(BF16)
    w['w_gate'] = w_in[:, RWKV_IN + MLA_IN:].astype(BF16)

    def lora_rows(mat, slot):
        z = jnp.zeros((LORA_SLAB, RWKV_WIDTH), F32)
        return lax.dynamic_update_slice(z, mat, (slot * DECAY_LORA, 0)).astype(BF16)

    w['w2f'] = lora_rows(p['rwkv_w2'][0], 0)
    w['w2b'] = lora_rows(p['rwkv_w2'][1], 1)
    w['a2f'] = lora_rows(p['rwkv_a2'][0], 2)
    w['a2b'] = lora_rows(p['rwkv_a2'][1], 3)
    w['g2'] = p['rwkv_g2'].astype(BF16)

    w_uq = p['mla_w_uq'].reshape(Q_LORA, MLA_HEADS, QK_DIM)
    w['wqn'] = w_uq[:, :, :QK_NOPE].reshape(Q_LORA, MLA_HEADS * QK_NOPE).astype(BF16)
    wqr = w_uq[:, :, QK_NOPE:]
    w['wqr'] = _rope_tile(wqr).reshape(Q_LORA, MLA_HEADS * ROPE_TILE).astype(BF16)
    w['wqs'] = _rope_tile(_swap_halves(wqr)).reshape(Q_LORA, MLA_HEADS * ROPE_TILE).astype(BF16)
    w_ukv = p['mla_w_ukv'].reshape(KV_LORA, MLA_HEADS, QK_NOPE + V_HEAD)
    w['wkn'] = w_ukv[:, :, :QK_NOPE].reshape(KV_LORA, MLA_HEADS * QK_NOPE).astype(BF16)
    w['wv'] = w_ukv[:, :, QK_NOPE:].reshape(KV_LORA, MLA_HEADS * V_HEAD).astype(BF16)
    w['qgr'] = _rope_tile(p['mla_q_gain'][QK_NOPE:])[None, :]
    w['qgs'] = _rope_tile(_swap_halves(p['mla_q_gain'][QK_NOPE:]))[None, :]
    w['kgr'] = _rope_tile(p['mla_k_gain'][QK_NOPE:])[None, :]
    w['kgs'] = _rope_tile(_swap_halves(p['mla_k_gain'][QK_NOPE:]))[None, :]
    w['w_a'] = p['w_branch_a'].astype(BF16)
    w['w_b'] = p['w_branch_b'].astype(BF16)
    w['w_o'] = p['w_out'].astype(BF16)
    return w


def _rope_tables(t):
    inv = ROPE_BASE ** (-jnp.arange(0, QK_ROPE, 2, dtype=F32) / QK_ROPE)
    ang = jnp.arange(t, dtype=F32)[:, None] * inv[None, :]
    cos, sin = jnp.cos(ang), jnp.sin(ang)
    return _rope_tile(jnp.concatenate([cos, cos], axis=1)), _rope_tile(jnp.concatenate([-sin, sin], axis=1))


def _row(vec):
    return vec.reshape(1, -1)


def _encoder_layer(x, p, w):
    b, t, d = x.shape
    m = b * t
    x0 = x.reshape(m, d)
    x1 = _ffn(x0, _row(p['ffn1_norm']), w['ffn1_w_in'], w['ffn1_w_out'])
    mix_g = _row(p['mix_norm'])
    z_rwkv = _norm_matmul(x1, mix_g, w['w_rwkv'], tm=256, tn=RWKV_IN, name="in_proj_rwkv")
    gates = _norm_matmul(x1, mix_g, w['w_gate'], tm=1024, tn=1024, name="in_proj_gate", gate=True)

    (r, v, kk, g, bonus, lw_f, kd_f, as_f, lw_b, kd_b, as_b) = _rwkv_prep(
        z_rwkv.reshape(b, t, RWKV_IN), _row(p['rwkv_mu']), p['rwkv_w0'], p['rwkv_a0'],
        w['w2f'], w['w2b'], w['a2f'], w['a2b'], w['g2'],
        _row(p['rwkv_k_k']), _row(p['rwkv_k_a']), _row(p['rwkv_r_k']))
    wkv_f, wkv_b = _wkv_scan(r, v, kk, lw_f, kd_f, as_f, lw_b, kd_b, as_b)

    cos2, sin2 = _rope_tables(t)
    q, k, val = _mla_prep(
        x1.reshape(b, t, d), mix_g, w['w_mla'], _row(p['mla_q_norm']), _row(p['mla_kv_norm']),
        w['wqn'], w['wqr'], w['wqs'], w['wkn'], w['wv'],
        _row(p['mla_q_gain']), w['qgr'], w['qgs'], _row(p['mla_k_gain']), w['kgr'], w['kgs'], cos2, sin2)
    o_b = _attention(q, k, val)

    flat = lambda u: u.reshape(m, -1)
    x2 = _merge(x1, flat(wkv_f), flat(wkv_b), flat(bonus), flat(g), flat(o_b), gates,
                _row(p['rwkv_ln_g']), _row(p['rwkv_ln_b']), w['w_a'], w['w_b'], w['w_o'])
    y = _ffn(x2, _row(p['ffn2_norm']), w['ffn2_w_in'], w['ffn2_w_out'], _row(p['out_norm']))
    return y.reshape(b, t, d)


def kernel(x_prompt, x_sample, ffn1_norm, ffn1_w_in, ffn1_w_out, mix_norm, w_in, rwkv_mu, rwkv_w0, rwkv_w2, rwkv_a0, rwkv_a2, rwkv_g2, rwkv_k_k, rwkv_k_a, rwkv_r_k, rwkv_ln_g, rwkv_ln_b, mla_q_norm, mla_w_uq, mla_kv_norm, mla_w_ukv, mla_q_gain, mla_k_gain, w_branch_a, w_branch_b, w_out, ffn2_norm, ffn2_w_in, ffn2_w_out, out_norm):
    stacked = dict(
        ffn1_norm=ffn1_norm, ffn1_w_in=ffn1_w_in, ffn1_w_out=ffn1_w_out, mix_norm=mix_norm, w_in=w_in,
        rwkv_mu=rwkv_mu, rwkv_w0=rwkv_w0, rwkv_w2=rwkv_w2, rwkv_a0=rwkv_a0, rwkv_a2=rwkv_a2,
        rwkv_g2=rwkv_g2, rwkv_k_k=rwkv_k_k, rwkv_k_a=rwkv_k_a, rwkv_r_k=rwkv_r_k,
        rwkv_ln_g=rwkv_ln_g, rwkv_ln_b=rwkv_ln_b, mla_q_norm=mla_q_norm, mla_w_uq=mla_w_uq,
        mla_kv_norm=mla_kv_norm, mla_w_ukv=mla_w_ukv, mla_q_gain=mla_q_gain, mla_k_gain=mla_k_gain,
        w_branch_a=w_branch_a, w_branch_b=w_branch_b, w_out=w_out, ffn2_norm=ffn2_norm,
        ffn2_w_in=ffn2_w_in, ffn2_w_out=ffn2_w_out, out_norm=out_norm)
    y_prompt, y_sample = x_prompt, x_sample
    for layer in range(ffn1_norm.shape[0]):
        p = {name: arr[layer] for name, arr in stacked.items()}
        p['rwkv_r_k'] = p['rwkv_r_k'].reshape(-1)
        w = _prepare_weights(p)
        y_prompt = _encoder_layer(y_prompt, p, w)
        y_sample = _encoder_layer(y_sample, p, w)
    return (y_prompt, y_sample)
```

```python
import functools
import math

import jax
import jax.numpy as jnp
from jax import lax
from jax.experimental import pallas as pl
from jax.experimental.pallas import tpu as pltpu

F32 = jnp.float32
BF16 = jnp.bfloat16

D_MODEL = 2048
D_FF = 5632
NORM_EPS = 1e-6

RWKV_HEADS = 16
RWKV_HEAD_DIM = 64
RWKV_WIDTH = RWKV_HEADS * RWKV_HEAD_DIM
DECAY_LORA = 96
AAA_LORA = 96
GATE_LORA = 256
LORA_SLAB = 2 * DECAY_LORA + 2 * AAA_LORA
GN_EPS = 64e-5
RWKV_IN = 3 * RWKV_WIDTH + LORA_SLAB + GATE_LORA

MLA_HEADS = 8
Q_LORA = 512
KV_LORA = 512
QK_NOPE = 128
QK_ROPE = 64
QK_DIM = QK_NOPE + QK_ROPE
V_HEAD = 128
MLA_WIDTH = MLA_HEADS * V_HEAD
MLA_IN = Q_LORA + KV_LORA + QK_ROPE
ROPE_TILE = 128
MLA_COLS = Q_LORA + KV_LORA + 2 * ROPE_TILE
ROPE_BASE = 10000.0
GATE_IN = 2 * D_MODEL

LANES = 128
SUBLANES = 8
MXU_DIM = 256
HALO_ROWS = 2 * SUBLANES
VMEM_LIMIT_BYTES = 56 * 1024 * 1024

WKV_CHUNK = 64
WKV_DIAG = 16
WKV_PARTS = 1
WKV_PAIRS_PER_STEP = 8
WKV_STAGGER = 2


def _params(*semantics):
    return pltpu.CompilerParams(dimension_semantics=semantics, vmem_limit_bytes=VMEM_LIMIT_BYTES)


def _const_spec(shape):
    zeros = (0,) * len(shape)
    return pl.BlockSpec(shape, lambda *_: zeros, pipeline_mode=pl.Buffered(1))


def _rms(x, g):
    ms = jnp.mean(x * x, axis=-1, keepdims=True)
    return x * lax.rsqrt(ms + NORM_EPS) * g


def _sigmoid(x):
    return 0.5 * jnp.tanh(0.5 * x) + 0.5


def _bf16_parts(x, n):
    if x.dtype == BF16:
        return [x]
    parts = []
    rem = x
    for i in range(n):
        p = rem.astype(BF16)
        parts.append(p)
        if i + 1 < n:
            rem = rem - p.astype(F32)
    return parts


_NN = (((1,), (0,)), ((), ()))
_NT = (((1,), (1,)), ((), ()))
_TN = (((0,), (0,)), ((), ()))


def _dotp(a, b, dims=_NN, pa=1, pb=1):
    a_parts = _bf16_parts(a, pa)
    b_parts = _bf16_parts(b, pb)
    order = max(len(a_parts), len(b_parts)) - 1
    acc = None
    for i, ap in enumerate(a_parts):
        for j, bp in enumerate(b_parts):
            if i + j > order:
                continue
            t = lax.dot_general(ap, bp, dims, preferred_element_type=F32)
            acc = t if acc is None else acc + t
    return acc


def _ffn_kernel(x_ref, g_ref, wg_ref, wu_ref, wo_ref, *rest, final_norm):
    if final_norm:
        gf_ref, o_ref, xn_ref = rest
    else:
        o_ref, xn_ref = rest
    j = pl.program_id(1)

    @pl.when(j == 0)
    def _():
        xn_ref[...] = _rms(x_ref[...], g_ref[...]).astype(BF16)
        o_ref[...] = jnp.zeros_like(o_ref)

    xn = xn_ref[...]
    gate = jnp.dot(xn, wg_ref[...], preferred_element_type=F32)
    up = jnp.dot(xn, wu_ref[...], preferred_element_type=F32)
    h = (gate * _sigmoid(gate) * up).astype(BF16)
    o_ref[...] += jnp.dot(h, wo_ref[...], preferred_element_type=F32)

    @pl.when(j == pl.num_programs(1) - 1)
    def _():
        y = x_ref[...] + 0.5 * o_ref[...]
        if final_norm:
            y = _rms(y, gf_ref[...])
        o_ref[...] = y


def _ffn(x, g, w_in, w_out, g_final=None, *, tm=512, tf=512):
    m, d = x.shape
    n_ff = w_out.shape[0]
    nj = n_ff // tf
    in_specs = [
        pl.BlockSpec((tm, d), lambda i, j: (i, 0)),
        pl.BlockSpec((1, d), lambda i, j: (0, 0)),
        pl.BlockSpec((d, tf), lambda i, j: (0, j)),
        pl.BlockSpec((d, tf), lambda i, j: (0, j + nj)),
        pl.BlockSpec((tf, d), lambda i, j: (j, 0)),
    ]
    args = [x, g, w_in, w_in, w_out]
    if g_final is not None:
        in_specs.append(pl.BlockSpec((1, d), lambda i, j: (0, 0)))
        args.append(g_final)
    return pl.pallas_call(
        functools.partial(_ffn_kernel, final_norm=g_final is not None),
        out_shape=jax.ShapeDtypeStruct((m, d), F32),
        grid=(m // tm, nj),
        in_specs=in_specs,
        out_specs=pl.BlockSpec((tm, d), lambda i, j: (i, 0)),
        scratch_shapes=[pltpu.VMEM((tm, d), BF16)],
        compiler_params=_params("parallel", "arbitrary"),
        name="ffn",
    )(*args)


def _norm_matmul_kernel(x_ref, g_ref, w_ref, o_ref, xn_ref, *, gate):
    @pl.when(pl.program_id(1) == 0)
    def _():
        xn_ref[...] = _rms(x_ref[...], g_ref[...]).astype(BF16)

    z = jnp.dot(xn_ref[...], w_ref[...], preferred_element_type=F32)
    o_ref[...] = (_sigmoid(z) if gate else z).astype(o_ref.dtype)


def _norm_matmul(x, g, w, *, tm, tn, name, gate=False):
    m, d = x.shape
    n = w.shape[1]
    w_spec = _const_spec((d, n)) if tn == n else pl.BlockSpec((d, tn), lambda i, j: (0, j))
    return pl.pallas_call(
        functools.partial(_norm_matmul_kernel, gate=gate),
        out_shape=jax.ShapeDtypeStruct((m, n), BF16),
        grid=(m // tm, n // tn),
        in_specs=[
            pl.BlockSpec((tm, d), lambda i, j: (i, 0)),
            pl.BlockSpec((1, d), lambda i, j: (0, 0)),
            w_spec,
        ],
        out_specs=pl.BlockSpec((tm, tn), lambda i, j: (i, j)),
        scratch_shapes=[pltpu.VMEM((tm, d), BF16)],
        compiler_params=_params("parallel", "arbitrary"),
        name=name,
    )(x, g, w)


def _head_sum(x):
    wide = MXU_DIM
    r = lax.broadcasted_iota(jnp.int32, (wide, wide), 0) // RWKV_HEAD_DIM
    c = lax.broadcasted_iota(jnp.int32, (wide, wide), 1) // RWKV_HEAD_DIM
    ones = (r == c).astype(BF16)
    tiles = [
        _dotp(x[:, j * wide:(j + 1) * wide], ones, pa=1, pb=1)
        for j in range(x.shape[1] // wide)
    ]
    return jnp.concatenate(tiles, axis=1)


def _rwkv_prep_kernel(z_ref, zp_ref, zn_ref, mu_ref, w0_ref, a0_ref, w2f_ref, w2b_ref, a2f_ref,
                      a2b_ref, g2_ref, kk_ref, ka_ref, rk_ref,
                      r_out, v_out, kkn_out, g_out, bonus_out,
                      lwf_out, kdf_out, asf_out, lwb_out, kdb_out, asb_out):
    j = pl.program_id(1)
    nj = pl.num_programs(1)
    z = z_ref[0].astype(F32)
    tm = z.shape[0]
    row = lax.broadcasted_iota(jnp.int32, z.shape, 0)
    halo_prev = jnp.where(j > 0, zp_ref[0, HALO_ROWS - 1:HALO_ROWS, :].astype(F32), 0.0)
    halo_next = jnp.where(j < nj - 1, zn_ref[0, 0:1, :].astype(F32), 0.0)
    prev = jnp.where(row == 0, halo_prev, pltpu.roll(z, 1, axis=0))
    nxt = jnp.where(row == tm - 1, halo_next, pltpu.roll(z, tm - 1, axis=0))
    zs = z + mu_ref[...] * (0.5 * (prev + nxt) - z)

    w = RWKV_WIDTH
    r = zs[:, 0:w]
    k = zs[:, w:2 * w]
    v = zs[:, 2 * w:3 * w]
    lora = zs[:, 3 * w:3 * w + LORA_SLAB]
    gd = zs[:, 3 * w + LORA_SLAB:]

    g_out[0] = jnp.dot(_sigmoid(gd).astype(BF16), g2_ref[...], preferred_element_type=F32).astype(g_out.dtype)
    kk = k * kk_ref[...]
    kk = kk * lax.rsqrt(_head_sum(kk * kk) + 1e-12)
    r_out[0] = r.astype(r_out.dtype)
    v_out[0] = v.astype(v_out.dtype)
    kkn_out[0] = kk.astype(kkn_out.dtype)

    lora_t = jnp.tanh(lora).astype(BF16)
    lora_l = lora.astype(BF16)
    bonus = None
    for d, (w2_ref, a2_ref, lw_out, kd_out, as_out) in enumerate(
            ((w2f_ref, a2f_ref, lwf_out, kdf_out, asf_out), (w2b_ref, a2b_ref, lwb_out, kdb_out, asb_out))):
        w_pre = w0_ref[d:d + 1, :] + jnp.dot(lora_t, w2_ref[...], preferred_element_type=F32)
        lw_out[0] = -math.exp(-0.5) * _sigmoid(w_pre)
        a = _sigmoid(a0_ref[d:d + 1, :] + jnp.dot(lora_l, a2_ref[...], preferred_element_type=F32))
        k_d = k * (1.0 + (a - 1.0) * ka_ref[...])
        kd_out[0] = k_d.astype(kd_out.dtype)
        as_out[0] = a.astype(as_out.dtype)
        b_d = _head_sum(r * k_d * rk_ref[...]) * v
        bonus = b_d if bonus is None else bonus + b_d
    bonus_out[0] = bonus.astype(bonus_out.dtype)


def _rwkv_prep(z, mu, w0, a0, w2f, w2b, a2f, a2b, g2, k_k, k_a, r_k, *, tm=256):
    b, t, zc = z.shape
    w = RWKV_WIDTH
    nb = tm // HALO_ROWS
    last_halo = t // HALO_ROWS - 1
    narrow = jax.ShapeDtypeStruct((b, t, w), BF16)
    wide = jax.ShapeDtypeStruct((b, t, w), F32)
    row_spec = pl.BlockSpec((1, tm, w), lambda i, j: (i, j, 0))
    return pl.pallas_call(
        _rwkv_prep_kernel,
        out_shape=[narrow] * 5 + [wide, narrow, narrow] * 2,
        grid=(b, t // tm),
        in_specs=[
            pl.BlockSpec((1, tm, zc), lambda i, j: (i, j, 0)),
            pl.BlockSpec((1, HALO_ROWS, zc), lambda i, j: (i, jnp.maximum(j * nb - 1, 0), 0)),
            pl.BlockSpec((1, HALO_ROWS, zc), lambda i, j: (i, jnp.minimum((j + 1) * nb, last_halo), 0)),
            _const_spec((1, zc)),
            _const_spec((2, w)),
            _const_spec((2, w)),
            _const_spec((LORA_SLAB, w)),
            _const_spec((LORA_SLAB, w)),
            _const_spec((LORA_SLAB, w)),
            _const_spec((LORA_SLAB, w)),
            _const_spec((GATE_LORA, w)),
            _const_spec((1, w)),
            _const_spec((1, w)),
            _const_spec((1, w)),
        ],
        out_specs=[row_spec] * 11,
        compiler_params=_params("parallel", "parallel"),
        name="rwkv_prep",
    )(z, z, z, mu, w0, a0, w2f, w2b, a2f, a2b, g2, k_k, k_a, r_k)


def _wkv_chunk(r, lw, k, v, kk, asig, h_state, reverse):
    c = r.shape[0]
    n = 2 * c
    mm = functools.partial(_dotp, pa=WKV_PARTS, pb=WKV_PARTS)
    a = -kk
    b = kk * asig

    ti = lax.broadcasted_iota(jnp.int32, (c, c), 0)
    si = lax.broadcasted_iota(jnp.int32, (c, c), 1)
    tri = ((si >= ti) if reverse else (si <= ti)).astype(BF16)
    cum = _dotp(tri, lw, pa=1, pb=2)
    yield

    cum_x = cum - lw
    total = jnp.sum(lw, axis=0, keepdims=True)
    e_in = jnp.exp(cum)
    e_ex = jnp.exp(cum_x)
    e_neg = jnp.exp(-cum)
    e_rest = jnp.exp(total - cum)

    lane = lax.broadcasted_iota(jnp.int32, (c, LANES), 1)
    first = lane < RWKV_HEAD_DIM

    def stack(x):
        return jnp.concatenate([jnp.where(first, x, 0.0), jnp.where(first, 0.0, x)], axis=0)

    lhs = jnp.concatenate([stack(a * e_ex), stack(r * e_in)], axis=0)
    rhs = jnp.concatenate([stack(b * e_neg), stack(k * e_neg)], axis=0)
    both = mm(lhs, jnp.concatenate([rhs, h_state], axis=0), _NT)
    aa = both[:, :2 * n]
    x1 = both[:, 2 * n:]
    vs = stack(v)
    khv = mm(vs, stack(k * e_rest), _TN)
    yield

    row = lax.broadcasted_iota(jnp.int32, (n, n), 0)
    col = lax.broadcasted_iota(jnp.int32, (n, n), 1)
    t_in = row % c
    s_in = col % c
    strict = (s_in > t_in) if reverse else (s_in < t_in)
    incl = (s_in >= t_in) if reverse else (s_in <= t_in)
    n_ab = jnp.where(strict, aa[:n, :n], 0.0)
    a_ak = jnp.where(strict, aa[:n, n:], 0.0)
    a_rb = jnp.where(incl, aa[n:, :n], 0.0)
    a_rk = jnp.where(incl, aa[n:, n:], 0.0)

    eye = (row == col).astype(F32)
    diag = (row // WKV_DIAG) == (col // WKV_DIAG)
    n_d = jnp.where(diag, n_ab, 0.0)
    n_o = jnp.where(diag, 0.0, n_ab)
    x = eye + n_d
    pw = mm(n_d, n_d)
    av = mm(jnp.concatenate([a_ak, a_rk], axis=0), vs)
    yield
    for _ in range(int(math.log2(WKV_DIAG)) - 2):
        stacked = mm(jnp.concatenate([x, pw], axis=0), pw)
        x = x + stacked[:n]
        pw = stacked[n:]
        yield
    t_d = x + mm(x, pw)
    yield
    e1 = mm(t_d, n_o)
    yield
    y = eye + e1
    pw = e1
    for _ in range(int(math.log2(c // WKV_DIAG)) - 1):
        pw = mm(pw, pw)
        yield
        y = y + mm(y, pw)
        yield
    t_inv = mm(y, t_d)
    yield
    us = mm(t_inv, x1[:n] + av[:n])
    yield
    os_ = x1[n:] + mm(a_rb, us) + av[n:]
    out = os_[:c] + os_[c:]
    h_new = jnp.exp(total) * h_state + mm(us, stack(b * e_rest), _TN) + khv
    return out, h_new


def _run_lockstep(chains, delays):
    results = [None] * len(chains)
    live = list(range(len(chains)))
    rnd = 0
    while live:
        still = []
        for i in live:
            if rnd < delays[i]:
                still.append(i)
                continue
            try:
                next(chains[i])
                still.append(i)
            except StopIteration as stop:
                results[i] = stop.value
        live = still
        rnd += 1
    return results


def _wkv_kernel(rf_ref, vf_ref, kkf_ref, lwf_ref, kdf_ref, asf_ref,
                rb_ref, vb_ref, kkb_ref, lwb_ref, kdb_ref, asb_ref,
                of_ref, ob_ref, hf_ref, hb_ref):
    @pl.when(pl.program_id(2) == 0)
    def _():
        hf_ref[...] = jnp.zeros_like(hf_ref)
        hb_ref[...] = jnp.zeros_like(hb_ref)

    chains = []
    for g in range(hf_ref.shape[0]):
        ls = slice(g * LANES, (g + 1) * LANES)
        fwd = [ref[0, :, ls].astype(F32) for ref in (rf_ref, lwf_ref, kdf_ref, vf_ref, kkf_ref, asf_ref)]
        bwd = [ref[0, :, ls].astype(F32) for ref in (rb_ref, lwb_ref, kdb_ref, vb_ref, kkb_ref, asb_ref)]
        chains.append(_wkv_chunk(*fwd, hf_ref[g], reverse=False))
        chains.append(_wkv_chunk(*bwd, hb_ref[g], reverse=True))
    results = _run_lockstep(chains, [WKV_STAGGER * (i % 2) for i in range(len(chains))])
    for g in range(hf_ref.shape[0]):
        ls = slice(g * LANES, (g + 1) * LANES)
        (o_f, h_f), (o_b, h_b) = results[2 * g], results[2 * g + 1]
        of_ref[0, :, ls] = o_f.astype(of_ref.dtype)
        hf_ref[g] = h_f
        ob_ref[0, :, ls] = o_b.astype(ob_ref.dtype)
        hb_ref[g] = h_b


def _wkv_scan(r, v, kk, lw_f, kd_f, as_f, lw_b, kd_b, as_b):
    b, t, w = r.shape
    c = WKV_CHUNK
    nc = t // c
    pairs = min(WKV_PAIRS_PER_STEP, w // LANES)
    wb = pairs * LANES
    fwd = pl.BlockSpec((1, c, wb), lambda i, p, s: (i, s, p))
    bwd = pl.BlockSpec((1, c, wb), lambda i, p, s: (i, nc - 1 - s, p))
    out = jax.ShapeDtypeStruct((b, t, w), BF16)
    state = pltpu.VMEM((pairs, LANES, LANES), F32)
    return pl.pallas_call(
        _wkv_kernel,
        out_shape=[out, out],
        grid=(b, w // wb, nc),
        in_specs=[fwd] * 6 + [bwd] * 6,
        out_specs=[fwd, bwd],
        scratch_shapes=[state, state],
        compiler_params=_params("parallel", "parallel", "arbitrary"),
        name="wkv_scan",
    )(r, v, kk, lw_f, kd_f, as_f, r, v, kk, lw_b, kd_b, as_b)


def _mla_prep_kernel(x_ref, g_ref, wz_ref, qn_ref, kvn_ref, wqn_ref, wqr_ref, wqs_ref, wkn_ref, wv_ref,
                     qg_ref, qgr_ref, qgs_ref, kg_ref, kgr_ref, kgs_ref, cos_ref, sin_ref, q_out, k_out, v_out):
    xn = _rms(x_ref[0], g_ref[...]).astype(BF16)
    z = jnp.dot(xn, wz_ref[...], preferred_element_type=F32)
    cq = _rms(z[:, :Q_LORA], qn_ref[...]).astype(BF16)
    ckv = _rms(z[:, Q_LORA:Q_LORA + KV_LORA], kvn_ref[...]).astype(BF16)
    kr = z[:, MLA_COLS - 2 * ROPE_TILE:MLA_COLS - ROPE_TILE]
    krs = z[:, MLA_COLS - ROPE_TILE:]
    q_nope = jnp.dot(cq, wqn_ref[...], preferred_element_type=F32)
    q_rope = jnp.dot(cq, wqr_ref[...], preferred_element_type=F32)
    q_swap = jnp.dot(cq, wqs_ref[...], preferred_element_type=F32)
    k_nope = jnp.dot(ckv, wkn_ref[...], preferred_element_type=F32)
    val = jnp.dot(ckv, wv_ref[...], preferred_element_type=F32)
    cos = cos_ref[...]
    sin = sin_ref[...]
    scale = math.log2(math.e) / math.sqrt(QK_DIM)
    ones_v = jnp.ones((val.shape[0], V_HEAD), BF16)
    kr_ss = jnp.sum(kr * kr, axis=-1, keepdims=True)
    for h in range(MLA_HEADS):
        nope = slice(h * QK_NOPE, (h + 1) * QK_NOPE)
        rope = slice(h * ROPE_TILE, (h + 1) * ROPE_TILE)
        qn, qr, qs = q_nope[:, nope], q_rope[:, rope], q_swap[:, rope]
        ss = jnp.sum(qn * qn, axis=-1, keepdims=True) + jnp.sum(qr * qr, axis=-1, keepdims=True)
        rs = lax.rsqrt(ss / QK_DIM + NORM_EPS)
        q_out[0, h, :, :QK_NOPE] = (qn * rs * qg_ref[:, :QK_NOPE] * scale).astype(BF16)
        rot = qr * rs * qgr_ref[...] * cos + qs * rs * qgs_ref[...] * sin
        q_out[0, h, :, QK_NOPE:] = (rot[:, :QK_ROPE] * scale).astype(BF16)
        kn = k_nope[:, nope]
        ss = jnp.sum(kn * kn, axis=-1, keepdims=True) + kr_ss
        rs = lax.rsqrt(ss / QK_DIM + NORM_EPS)
        k_out[0, h, :, :QK_NOPE] = (kn * rs * kg_ref[:, :QK_NOPE]).astype(BF16)
        rot = kr * rs * kgr_ref[...] * cos + krs * rs * kgs_ref[...] * sin
        k_out[0, h, :, QK_NOPE:] = rot[:, :QK_ROPE].astype(BF16)
        v_out[0, h, :, :V_HEAD] = val[:, nope].astype(BF16)
        v_out[0, h, :, V_HEAD:] = ones_v


def _mla_prep(x, g, wz, q_norm, kv_norm, wqn, wqr, wqs, wkn, wv, qg, qgr, qgs, kg, kgr, kgs, cos2, sin2, *, tm=256):
    b, t, d = x.shape
    h = MLA_HEADS
    return pl.pallas_call(
        _mla_prep_kernel,
        out_shape=[
            jax.ShapeDtypeStruct((b, h, t, QK_DIM), BF16),
            jax.ShapeDtypeStruct((b, h, t, QK_DIM), BF16),
            jax.ShapeDtypeStruct((b, h, t, 2 * V_HEAD), BF16),
        ],
        grid=(b, t // tm),
        in_specs=[
            pl.BlockSpec((1, tm, d), lambda i, j: (i, j, 0)),
            _const_spec((1, d)),
            _const_spec(wz.shape),
            _const_spec((1, Q_LORA)),
            _const_spec((1, KV_LORA)),
            _const_spec(wqn.shape),
            _const_spec(wqr.shape),
            _const_spec(wqs.shape),
            _const_spec(wkn.shape),
            _const_spec(wv.shape),
            _const_spec((1, QK_DIM)),
            _const_spec((1, ROPE_TILE)),
            _const_spec((1, ROPE_TILE)),
            _const_spec((1, QK_DIM)),
            _const_spec((1, ROPE_TILE)),
            _const_spec((1, ROPE_TILE)),
            pl.BlockSpec((tm, ROPE_TILE), lambda i, j: (j, 0)),
            pl.BlockSpec((tm, ROPE_TILE), lambda i, j: (j, 0)),
        ],
        out_specs=[
            pl.BlockSpec((1, h, tm, QK_DIM), lambda i, j: (i, 0, j, 0)),
            pl.BlockSpec((1, h, tm, QK_DIM), lambda i, j: (i, 0, j, 0)),
            pl.BlockSpec((1, h, tm, 2 * V_HEAD), lambda i, j: (i, 0, j, 0)),
        ],
        compiler_params=_params("parallel", "parallel"),
        name="mla_prep",
    )(x, g, wz, q_norm, kv_norm, wqn, wqr, wqs, wkn, wv, qg, qgr, qgs, kg, kgr, kgs, cos2, sin2)


def _attn_kernel(q_ref, k_ref, v_ref, o_ref, s_scr, p_scr, rmax_scr, m_ref, acc_ref, *, sub):
    n_sub = k_ref.shape[2] // sub
    assert n_sub == 1 or n_sub % 2 == 0
    q = q_ref[0, 0]
    dv = o_ref.shape[2]

    def scores(j, slot):
        rows = pl.ds(pl.multiple_of(j * sub, sub), sub)
        s = lax.dot_general(q, k_ref[0, 0, rows, :], _NT, preferred_element_type=F32)
        s_scr[slot] = s
        rmax_scr[slot] = jnp.broadcast_to(jnp.max(s, axis=-1, keepdims=True), rmax_scr.shape[1:])

    def weighted(j, slot):
        rows = pl.ds(pl.multiple_of(j * sub, sub), sub)
        return jnp.dot(p_scr[slot], v_ref[0, 0, rows, :], preferred_element_type=F32)

    def softmax(slot):
        m_old = m_ref[...]
        m_new = jnp.maximum(m_old, rmax_scr[slot])
        alpha = jnp.exp2(m_old - m_new)
        m_ref[...] = m_new
        p_scr[slot] = jnp.exp2(s_scr[slot] - jnp.concatenate([m_new] * (sub // LANES), axis=1)).astype(BF16)
        return jnp.concatenate([alpha] * (acc_ref.shape[1] // LANES), axis=1)

    def stage(j, slot, first=False, last=False):
        if not last:
            scores(j + 1, 1 - slot)
        pv = None if first else weighted(j - 1, 1 - slot)
        alpha = softmax(slot)
        if not first:
            acc_ref[...] = alpha * (acc_ref[...] + pv)

    m_ref[...] = jnp.full_like(m_ref, -jnp.inf)
    acc_ref[...] = jnp.zeros_like(acc_ref)
    scores(0, 0)
    if n_sub == 1:
        stage(0, 0, first=True, last=True)
    else:
        stage(0, 0, first=True)

        for j in range(1, n_sub - 1):
            stage(j, j % 2)
        stage(n_sub - 1, 1, last=True)
    acc = acc_ref[...] + weighted(n_sub - 1, (n_sub - 1) % 2)
    o_ref[0] = (acc[:, :dv] / acc[:, dv:]).astype(o_ref.dtype)


def _attention(q, k, v, *, tq=512, sub=2048):
    b, h, t, dq = q.shape
    dv = v.shape[-1] // 2
    tq = min(tq, t)
    sub = min(sub, t // 2)
    return pl.pallas_call(
        functools.partial(_attn_kernel, sub=sub),
        out_shape=jax.ShapeDtypeStruct((b, t, h * dv), BF16),
        grid=(b, h, t // tq),
        in_specs=[
            pl.BlockSpec((1, 1, tq, dq), lambda i, j, qi: (i, j, qi, 0)),
            pl.BlockSpec((1, 1, t, dq), lambda i, j, qi: (i, j, 0, 0)),
            pl.BlockSpec((1, 1, t, 2 * dv), lambda i, j, qi: (i, j, 0, 0)),
        ],
        out_specs=pl.BlockSpec((1, tq, dv), lambda i, j, qi: (i, qi, j)),
        scratch_shapes=[
            pltpu.VMEM((2, tq, sub), F32),
            pltpu.VMEM((2, tq, sub), BF16),
            pltpu.VMEM((2, tq, LANES), F32),
            pltpu.VMEM((tq, LANES), F32),
            pltpu.VMEM((tq, 2 * dv), F32),
        ],
        compiler_params=_params("parallel", "parallel", "arbitrary"),
        name="mla_attention",
    )(q, k, v)


def _group_norm(x, g, b):
    mean = _head_sum(x) * (1.0 / RWKV_HEAD_DIM)
    xc = x - mean
    var = _head_sum(xc * xc) * (1.0 / RWKV_HEAD_DIM)
    return xc * lax.rsqrt(var + GN_EPS) * g + b


def _merge_kernel(x_ref, wf_ref, wb_ref, bonus_ref, g_ref, ob_ref, ga_ref, gb_ref,
                  lng_ref, lnb_ref, wa_ref, wbr_ref, wo_ref, o_ref):
    lng = lng_ref[...]
    lnb = lnb_ref[...]
    gn = _group_norm(wf_ref[...].astype(F32), lng, lnb) + _group_norm(wb_ref[...].astype(F32), lng, lnb)
    o_a = (gn + bonus_ref[...].astype(F32)) * g_ref[...].astype(F32)
    br_a = jnp.dot(o_a.astype(BF16), wa_ref[...], preferred_element_type=F32)
    br_b = jnp.dot(ob_ref[...], wbr_ref[...], preferred_element_type=F32)
    merged = ga_ref[...].astype(F32) * br_a + gb_ref[...].astype(F32) * br_b
    o_ref[...] = x_ref[...] + jnp.dot(merged.astype(BF16), wo_ref[...], preferred_element_type=F32)


def _merge(x, wkv_f, wkv_b, bonus, g, o_b, gates, ln_g, ln_b, w_a, w_b, w_o, *, tm=256):
    m, d = x.shape
    w = RWKV_WIDTH
    row_w = pl.BlockSpec((tm, w), lambda i: (i, 0))
    return pl.pallas_call(
        _merge_kernel,
        out_shape=jax.ShapeDtypeStruct((m, d), F32),
        grid=(m // tm,),
        in_specs=[
            pl.BlockSpec((tm, d), lambda i: (i, 0)),
            row_w, row_w, row_w, row_w, row_w,
            pl.BlockSpec((tm, d), lambda i: (i, 0)),
            pl.BlockSpec((tm, d), lambda i: (i, 1)),
            _const_spec((1, w)),
            _const_spec((1, w)),
            _const_spec(w_a.shape),
            _const_spec(w_b.shape),
            _const_spec(w_o.shape),
        ],
        out_specs=pl.BlockSpec((tm, d), lambda i: (i, 0)),
        compiler_params=_params("parallel"),
        name="merge",
    )(x, wkv_f, wkv_b, bonus, g, o_b, gates, gates, ln_g, ln_b, w_a, w_b, w_o)


def _rope_tile(x):
    return jnp.pad(x, [(0, 0)] * (x.ndim - 1) + [(0, ROPE_TILE - x.shape[-1])])


def _swap_halves(x, axis=-1):
    lo, hi = jnp.split(x, 2, axis=axis)
    return jnp.concatenate([hi, lo], axis=axis)


def _prepare_weights(p):
    w = {}
    w['ffn1_w_in'] = p['ffn1_w_in'].astype(BF16)
    w['ffn1_w_out'] = p['ffn1_w_out'].astype(BF16)
    w['ffn2_w_in'] = p['ffn2_w_in'].astype(BF16)
    w['ffn2_w_out'] = p['ffn2_w_out'].astype(BF16)
    w_in = p['w_in']
    w['w_rwkv'] = w_in[:, :RWKV_IN].astype(BF16)
    mla = w_in[:, RWKV_IN:RWKV_IN + MLA_IN]
    w['w_mla'] = jnp.concatenate(
        [mla[:, :Q_LORA + KV_LORA], _rope_tile(mla[:, MLA_IN - QK_ROPE:]),
         _rope_tile(_swap_halves(mla[:, MLA_IN - QK_ROPE:]))], axis=1).astype(BF16)
    w['w_gate'] = w_in[:, RWKV_IN + MLA_IN:].astype(BF16)

    def lora_rows(mat, slot):
        z = jnp.zeros((LORA_SLAB, RWKV_WIDTH), F32)
        return lax.dynamic_update_slice(z, mat, (slot * DECAY_LORA, 0)).astype(BF16)

    w['w2f'] = lora_rows(p['rwkv_w2'][0], 0)
    w['w2b'] = lora_rows(p['rwkv_w2'][1], 1)
    w['a2f'] = lora_rows(p['rwkv_a2'][0], 2)
    w['a2b'] = lora_rows(p['rwkv_a2'][1], 3)
    w['g2'] = p['rwkv_g2'].astype(BF16)

    w_uq = p['mla_w_uq'].reshape(Q_LORA, MLA_HEADS, QK_DIM)
    w['wqn'] = w_uq[:, :, :QK_NOPE].reshape(Q_LORA, MLA_HEADS * QK_NOPE).astype(BF16)
    wqr = w_uq[:, :, QK_NOPE:]
    w['wqr'] = _rope_tile(wqr).reshape(Q_LORA, MLA_HEADS * ROPE_TILE).astype(BF16)
    w['wqs'] = _rope_tile(_swap_halves(wqr)).reshape(Q_LORA, MLA_HEADS * ROPE_TILE).astype(BF16)
    w_ukv = p['mla_w_ukv'].reshape(KV_LORA, MLA_HEADS, QK_NOPE + V_HEAD)
    w['wkn'] = w_ukv[:, :, :QK_NOPE].reshape(KV_LORA, MLA_HEADS * QK_NOPE).astype(BF16)
    w['wv'] = w_ukv[:, :, QK_NOPE:].reshape(KV_LORA, MLA_HEADS * V_HEAD).astype(BF16)
    w['qgr'] = _rope_tile(p['mla_q_gain'][QK_NOPE:])[None, :]
    w['qgs'] = _rope_tile(_swap_halves(p['mla_q_gain'][QK_NOPE:]))[None, :]
    w['kgr'] = _rope_tile(p['mla_k_gain'][QK_NOPE:])[None, :]
    w['kgs'] = _rope_tile(_swap_halves(p['mla_k_gain'][QK_NOPE:]))[None, :]
    w['w_a'] = p['w_branch_a'].astype(BF16)
    w['w_b'] = p['w_branch_b'].astype(BF16)
    w['w_o'] = p['w_out'].astype(BF16)
    return w


def _rope_tables(t):
    inv = ROPE_BASE ** (-jnp.arange(0, QK_ROPE, 2, dtype=F32) / QK_ROPE)
    ang = jnp.arange(t, dtype=F32)[:, None] * inv[None, :]
    cos, sin = jnp.cos(ang), jnp.sin(ang)
    return _rope_tile(jnp.concatenate([cos, cos], axis=1)), _rope_tile(jnp.concatenate([-sin, sin], axis=1))


def _row(vec):
    return vec.reshape(1, -1)


def _encoder_layer(x, p, w):
    b, t, d = x.shape
    m = b * t
    x0 = x.reshape(m, d)
    x1 = _ffn(x0, _row(p['ffn1_norm']), w['ffn1_w_in'], w['ffn1_w_out'])
    mix_g = _row(p['mix_norm'])
    z_rwkv = _norm_matmul(x1, mix_g, w['w_rwkv'], tm=256, tn=RWKV_IN, name="in_proj_rwkv")
    gates = _norm_matmul(x1, mix_g, w['w_gate'], tm=1024, tn=2048, name="in_proj_gate", gate=True)

    (r, v, kk, g, bonus, lw_f, kd_f, as_f, lw_b, kd_b, as_b) = _rwkv_prep(
        z_rwkv.reshape(b, t, RWKV_IN), _row(p['rwkv_mu']), p['rwkv_w0'], p['rwkv_a0'],
        w['w2f'], w['w2b'], w['a2f'], w['a2b'], w['g2'],
        _row(p['rwkv_k_k']), _row(p['rwkv_k_a']), _row(p['rwkv_r_k']))
    wkv_f, wkv_b = _wkv_scan(r, v, kk, lw_f, kd_f, as_f, lw_b, kd_b, as_b)

    cos2, sin2 = _rope_tables(t)
    q, k, val = _mla_prep(
        x1.reshape(b, t, d), mix_g, w['w_mla'], _row(p['mla_q_norm']), _row(p['mla_kv_norm']),
        w['wqn'], w['wqr'], w['wqs'], w['wkn'], w['wv'],
        _row(p['mla_q_gain']), w['qgr'], w['qgs'], _row(p['mla_k_gain']), w['kgr'], w['kgs'], cos2, sin2)
    o_b = _attention(q, k, val)

    flat = lambda u: u.reshape(m, -1)
    x2 = _merge(x1, flat(wkv_f), flat(wkv_b), flat(bonus), flat(g), flat(o_b), gates,
                _row(p['rwkv_ln_g']), _row(p['rwkv_ln_b']), w['w_a'], w['w_b'], w['w_o'])
    y = _ffn(x2, _row(p['ffn2_norm']), w['ffn2_w_in'], w['ffn2_w_out'], _row(p['out_norm']))
    return y.reshape(b, t, d)


def kernel(x_prompt, x_sample, ffn1_norm, ffn1_w_in, ffn1_w_out, mix_norm, w_in, rwkv_mu, rwkv_w0, rwkv_w2, rwkv_a0, rwkv_a2, rwkv_g2, rwkv_k_k, rwkv_k_a, rwkv_r_k, rwkv_ln_g, rwkv_ln_b, mla_q_norm, mla_w_uq, mla_kv_norm, mla_w_ukv, mla_q_gain, mla_k_gain, w_branch_a, w_branch_b, w_out, ffn2_norm, ffn2_w_in, ffn2_w_out, out_norm):
    stacked = dict(
        ffn1_norm=ffn1_norm, ffn1_w_in=ffn1_w_in, ffn1_w_out=ffn1_w_out, mix_norm=mix_norm, w_in=w_in,
        rwkv_mu=rwkv_mu, rwkv_w0=rwkv_w0, rwkv_w2=rwkv_w2, rwkv_a0=rwkv_a0, rwkv_a2=rwkv_a2,
        rwkv_g2=rwkv_g2, rwkv_k_k=rwkv_k_k, rwkv_k_a=rwkv_k_a, rwkv_r_k=rwkv_r_k,
        rwkv_ln_g=rwkv_ln_g, rwkv_ln_b=rwkv_ln_b, mla_q_norm=mla_q_norm, mla_w_uq=mla_w_uq,
        mla_kv_norm=mla_kv_norm, mla_w_ukv=mla_w_ukv, mla_q_gain=mla_q_gain, mla_k_gain=mla_k_gain,
        w_branch_a=w_branch_a, w_branch_b=w_branch_b, w_out=w_out, ffn2_norm=ffn2_norm,
        ffn2_w_in=ffn2_w_in, ffn2_w_out=ffn2_w_out, out_norm=out_norm)
    y_prompt, y_sample = x_prompt, x_sample
    for layer in range(ffn1_norm.shape[0]):
        p = {name: arr[layer] for name, arr in stacked.items()}
        p['rwkv_r_k'] = p['rwkv_r_k'].reshape(-1)
        w = _prepare_weights(p)
        y_prompt = _encoder_layer(y_prompt, p, w)
        y_sample = _encoder_layer(y_sample, p, w)
    return (y_prompt, y_sample)
```

```python
import functools
import math

import jax
import jax.numpy as jnp
from jax import lax
from jax.experimental import pallas as pl
from jax.experimental.pallas import tpu as pltpu

F32 = jnp.float32
BF16 = jnp.bfloat16

NORM_EPS = 1e-6

LANES = 128
SUBLANES = 8
MXU_DIM = 256
HALO_ROWS = 2 * SUBLANES
VMEM_LIMIT_BYTES = 56 * 1024 * 1024

RWKV_HEADS = 16
RWKV_HEAD_DIM = 64
RWKV_WIDTH = RWKV_HEADS * RWKV_HEAD_DIM
DECAY_LORA = 96
AAA_LORA = 96
GATE_LORA = 256
LORA_SLAB = 2 * DECAY_LORA + 2 * AAA_LORA
GN_EPS = 64e-5
RWKV_IN = 3 * RWKV_WIDTH + LORA_SLAB + GATE_LORA

MLA_HEADS = 8
Q_LORA = 512
KV_LORA = 512
QK_NOPE = 128
QK_ROPE = 64
QK_DIM = QK_NOPE + QK_ROPE
V_HEAD = 128
MLA_IN = Q_LORA + KV_LORA + QK_ROPE
ROPE_TILE = LANES
MLA_COLS = Q_LORA + KV_LORA + 2 * ROPE_TILE
ROPE_BASE = 10000.0

WKV_CHUNK = 64
WKV_DIAG = 16
WKV_PARTS = 1
WKV_PAIRS_PER_STEP = 8
WKV_STAGGER = 2


def _params(*semantics):
    return pltpu.CompilerParams(dimension_semantics=semantics, vmem_limit_bytes=VMEM_LIMIT_BYTES)


def _const_spec(shape):
    zeros = (0,) * len(shape)
    return pl.BlockSpec(shape, lambda *_: zeros, pipeline_mode=pl.Buffered(1))


def _rms(x, g):
    ms = jnp.mean(x * x, axis=-1, keepdims=True)
    return x * lax.rsqrt(ms + NORM_EPS) * g


def _sigmoid(x):
    return 0.5 * jnp.tanh(0.5 * x) + 0.5


def _bf16_parts(x, n):
    if x.dtype == BF16:
        return [x]
    parts = []
    rem = x
    for i in range(n):
        p = rem.astype(BF16)
        parts.append(p)
        if i + 1 < n:
            rem = rem - p.astype(F32)
    return parts


_NN = (((1,), (0,)), ((), ()))
_NT = (((1,), (1,)), ((), ()))
_TN = (((0,), (0,)), ((), ()))


def _dotp(a, b, dims=_NN, pa=1, pb=1):
    a_parts = _bf16_parts(a, pa)
    b_parts = _bf16_parts(b, pb)
    order = max(len(a_parts), len(b_parts)) - 1
    acc = None
    for i, ap in enumerate(a_parts):
        for j, bp in enumerate(b_parts):
            if i + j > order:
                continue
            t = lax.dot_general(ap, bp, dims, preferred_element_type=F32)
            acc = t if acc is None else acc + t
    return acc


def _ffn_kernel(x_ref, g_ref, wg_ref, wu_ref, wo_ref, *rest, final_norm):
    if final_norm:
        gf_ref, o_ref, xn_ref = rest
    else:
        o_ref, xn_ref = rest
    j = pl.program_id(1)

    @pl.when(j == 0)
    def _():
        xn_ref[...] = _rms(x_ref[...], g_ref[...]).astype(BF16)
        o_ref[...] = jnp.zeros_like(o_ref)

    xn = xn_ref[...]
    gate = jnp.dot(xn, wg_ref[...], preferred_element_type=F32)
    up = jnp.dot(xn, wu_ref[...], preferred_element_type=F32)
    h = (gate * _sigmoid(gate) * up).astype(BF16)
    o_ref[...] += jnp.dot(h, wo_ref[...], preferred_element_type=F32)

    @pl.when(j == pl.num_programs(1) - 1)
    def _():
        y = x_ref[...] + 0.5 * o_ref[...]
        if final_norm:
            y = _rms(y, gf_ref[...])
        o_ref[...] = y


def _ffn(x, g, w_in, w_out, g_final=None, *, tm=512, tf=512):
    m, d = x.shape
    n_ff = w_out.shape[0]
    nj = n_ff // tf
    in_specs = [
        pl.BlockSpec((tm, d), lambda i, j: (i, 0)),
        pl.BlockSpec((1, d), lambda i, j: (0, 0)),
        pl.BlockSpec((d, tf), lambda i, j: (0, j)),
        pl.BlockSpec((d, tf), lambda i, j: (0, j + nj)),
        pl.BlockSpec((tf, d), lambda i, j: (j, 0)),
    ]
    args = [x, g, w_in, w_in, w_out]
    if g_final is not None:
        in_specs.append(pl.BlockSpec((1, d), lambda i, j: (0, 0)))
        args.append(g_final)
    return pl.pallas_call(
        functools.partial(_ffn_kernel, final_norm=g_final is not None),
        out_shape=jax.ShapeDtypeStruct((m, d), F32),
        grid=(m // tm, nj),
        in_specs=in_specs,
        out_specs=pl.BlockSpec((tm, d), lambda i, j: (i, 0)),
        scratch_shapes=[pltpu.VMEM((tm, d), BF16)],
        compiler_params=_params("parallel", "arbitrary"),
        name="ffn",
    )(*args)


def _norm_matmul_kernel(x_ref, g_ref, w_ref, o_ref, xn_ref, *, gate):
    @pl.when(pl.program_id(1) == 0)
    def _():
        xn_ref[...] = _rms(x_ref[...], g_ref[...]).astype(BF16)

    z = jnp.dot(xn_ref[...], w_ref[...], preferred_element_type=F32)
    o_ref[...] = (_sigmoid(z) if gate else z).astype(o_ref.dtype)


def _norm_matmul(x, g, w, *, tm, tn, name, gate=False):
    m, d = x.shape
    n = w.shape[1]
    w_spec = _const_spec((d, n)) if tn == n else pl.BlockSpec((d, tn), lambda i, j: (0, j))
    return pl.pallas_call(
        functools.partial(_norm_matmul_kernel, gate=gate),
        out_shape=jax.ShapeDtypeStruct((m, n), BF16),
        grid=(m // tm, n // tn),
        in_specs=[
            pl.BlockSpec((tm, d), lambda i, j: (i, 0)),
            pl.BlockSpec((1, d), lambda i, j: (0, 0)),
            w_spec,
        ],
        out_specs=pl.BlockSpec((tm, tn), lambda i, j: (i, j)),
        scratch_shapes=[pltpu.VMEM((tm, d), BF16)],
        compiler_params=_params("parallel", "arbitrary"),
        name=name,
    )(x, g, w)


def _head_sum(x):
    wide = MXU_DIM
    r = lax.broadcasted_iota(jnp.int32, (wide, wide), 0) // RWKV_HEAD_DIM
    c = lax.broadcasted_iota(jnp.int32, (wide, wide), 1) // RWKV_HEAD_DIM
    ones = (r == c).astype(BF16)
    tiles = [
        _dotp(x[:, j * wide:(j + 1) * wide], ones, pa=1, pb=1)
        for j in range(x.shape[1] // wide)
    ]
    return jnp.concatenate(tiles, axis=1)


def _rwkv_prep_kernel(z_ref, zp_ref, zn_ref, mu_ref, w0_ref, a0_ref, w2f_ref, w2b_ref, a2f_ref,
                      a2b_ref, g2_ref, kk_ref, ka_ref, rk_ref,
                      r_out, v_out, kkn_out, g_out, bonus_out,
                      lwf_out, kdf_out, asf_out, lwb_out, kdb_out, asb_out):
    j = pl.program_id(1)
    nj = pl.num_programs(1)
    z = z_ref[0].astype(F32)
    tm = z.shape[0]
    row = lax.broadcasted_iota(jnp.int32, z.shape, 0)
    halo_prev = jnp.where(j > 0, zp_ref[0, HALO_ROWS - 1:HALO_ROWS, :].astype(F32), 0.0)
    halo_next = jnp.where(j < nj - 1, zn_ref[0, 0:1, :].astype(F32), 0.0)
    prev = jnp.where(row == 0, halo_prev, pltpu.roll(z, 1, axis=0))
    nxt = jnp.where(row == tm - 1, halo_next, pltpu.roll(z, tm - 1, axis=0))
    zs = z + mu_ref[...] * (0.5 * (prev + nxt) - z)

    w = RWKV_WIDTH
    r = zs[:, 0:w]
    k = zs[:, w:2 * w]
    v = zs[:, 2 * w:3 * w]
    lora = zs[:, 3 * w:3 * w + LORA_SLAB]
    gd = zs[:, 3 * w + LORA_SLAB:]

    g_out[0] = jnp.dot(_sigmoid(gd).astype(BF16), g2_ref[...], preferred_element_type=F32).astype(g_out.dtype)
    kk = k * kk_ref[...]
    kk = kk * lax.rsqrt(_head_sum(kk * kk) + 1e-12)
    r_out[0] = r.astype(r_out.dtype)
    v_out[0] = v.astype(v_out.dtype)
    kkn_out[0] = kk.astype(kkn_out.dtype)

    lora_t = jnp.tanh(lora).astype(BF16)
    lora_l = lora.astype(BF16)
    bonus = None
    for d, (w2_ref, a2_ref, lw_out, kd_out, as_out) in enumerate(
            ((w2f_ref, a2f_ref, lwf_out, kdf_out, asf_out), (w2b_ref, a2b_ref, lwb_out, kdb_out, asb_out))):
        w_pre = w0_ref[d:d + 1, :] + jnp.dot(lora_t, w2_ref[...], preferred_element_type=F32)
        lw_out[0] = -math.exp(-0.5) * _sigmoid(w_pre)
        a = _sigmoid(a0_ref[d:d + 1, :] + jnp.dot(lora_l, a2_ref[...], preferred_element_type=F32))
        k_d = k * (1.0 + (a - 1.0) * ka_ref[...])
        kd_out[0] = k_d.astype(kd_out.dtype)
        as_out[0] = a.astype(as_out.dtype)
        b_d = _head_sum(r * k_d * rk_ref[...]) * v
        bonus = b_d if bonus is None else bonus + b_d
    bonus_out[0] = bonus.astype(bonus_out.dtype)


def _rwkv_prep(z, mu, w0, a0, w2f, w2b, a2f, a2b, g2, k_k, k_a, r_k, *, tm=256):
    b, t, zc = z.shape
    w = RWKV_WIDTH
    nb = tm // HALO_ROWS
    last_halo = t // HALO_ROWS - 1
    narrow = jax.ShapeDtypeStruct((b, t, w), BF16)
    wide = jax.ShapeDtypeStruct((b, t, w), F32)
    row_spec = pl.BlockSpec((1, tm, w), lambda i, j: (i, j, 0))
    return pl.pallas_call(
        _rwkv_prep_kernel,
        out_shape=[narrow] * 5 + [wide, narrow, narrow] * 2,
        grid=(b, t // tm),
        in_specs=[
            pl.BlockSpec((1, tm, zc), lambda i, j: (i, j, 0)),
            pl.BlockSpec((1, HALO_ROWS, zc), lambda i, j: (i, jnp.maximum(j * nb - 1, 0), 0)),
            pl.BlockSpec((1, HALO_ROWS, zc), lambda i, j: (i, jnp.minimum((j + 1) * nb, last_halo), 0)),
            _const_spec((1, zc)),
            _const_spec((2, w)),
            _const_spec((2, w)),
            _const_spec((LORA_SLAB, w)),
            _const_spec((LORA_SLAB, w)),
            _const_spec((LORA_SLAB, w)),
            _const_spec((LORA_SLAB, w)),
            _const_spec((GATE_LORA, w)),
            _const_spec((1, w)),
            _const_spec((1, w)),
            _const_spec((1, w)),
        ],
        out_specs=[row_spec] * 11,
        compiler_params=_params("parallel", "parallel"),
        name="rwkv_prep",
    )(z, z, z, mu, w0, a0, w2f, w2b, a2f, a2b, g2, k_k, k_a, r_k)


def _wkv_chunk(r, lw, k, v, kk, asig, h_state, reverse):
    c = r.shape[0]
    n = 2 * c
    mm = functools.partial(_dotp, pa=WKV_PARTS, pb=WKV_PARTS)
    a = -kk
    b = kk * asig

    ti = lax.broadcasted_iota(jnp.int32, (c, c), 0)
    si = lax.broadcasted_iota(jnp.int32, (c, c), 1)
    tri = ((si >= ti) if reverse else (si <= ti)).astype(BF16)
    cum = _dotp(tri, lw, pa=1, pb=2)
    yield

    cum_x = cum - lw
    total = jnp.sum(lw, axis=0, keepdims=True)
    e_in = jnp.exp(cum)
    e_ex = jnp.exp(cum_x)
    e_neg = jnp.exp(-cum)
    e_rest = jnp.exp(total - cum)

    lane = lax.broadcasted_iota(jnp.int32, (c, LANES), 1)
    first = lane < RWKV_HEAD_DIM

    def stack(x):
        return jnp.concatenate([jnp.where(first, x, 0.0), jnp.where(first, 0.0, x)], axis=0)

    lhs = jnp.concatenate([stack(a * e_ex), stack(r * e_in)], axis=0)
    rhs = jnp.concatenate([stack(b * e_neg), stack(k * e_neg)], axis=0)
    both = mm(lhs, jnp.concatenate([rhs, h_state], axis=0), _NT)
    aa = both[:, :2 * n]
    x1 = both[:, 2 * n:]
    vs = stack(v)
    khv = mm(vs, stack(k * e_rest), _TN)
    yield

    row = lax.broadcasted_iota(jnp.int32, (n, n), 0)
    col = lax.broadcasted_iota(jnp.int32, (n, n), 1)
    t_in = row % c
    s_in = col % c
    strict = (s_in > t_in) if reverse else (s_in < t_in)
    incl = (s_in >= t_in) if reverse else (s_in <= t_in)
    n_ab = jnp.where(strict, aa[:n, :n], 0.0)
    a_ak = jnp.where(strict, aa[:n, n:], 0.0)
    a_rb = jnp.where(incl, aa[n:, :n], 0.0)
    a_rk = jnp.where(incl, aa[n:, n:], 0.0)

    eye = (row == col).astype(F32)
    diag = (row // WKV_DIAG) == (col // WKV_DIAG)
    n_d = jnp.where(diag, n_ab, 0.0)
    n_o = jnp.where(diag, 0.0, n_ab)
    x = eye + n_d
    pw = mm(n_d, n_d)
    av = mm(jnp.concatenate([a_ak, a_rk], axis=0), vs)
    yield
    for _ in range(int(math.log2(WKV_DIAG)) - 2):
        stacked = mm(jnp.concatenate([x, pw], axis=0), pw)
        x = x + stacked[:n]
        pw = stacked[n:]
        yield
    t_d = x + mm(x, pw)
    yield
    e1 = mm(t_d, n_o)
    yield
    y = eye + e1
    pw = e1
    for _ in range(int(math.log2(c // WKV_DIAG)) - 1):
        pw = mm(pw, pw)
        yield
        y = y + mm(y, pw)
        yield
    t_inv = mm(y, t_d)
    yield
    us = mm(t_inv, x1[:n] + av[:n])
    yield
    os_ = x1[n:] + mm(a_rb, us) + av[n:]
    out = os_[:c] + os_[c:]
    h_new = jnp.exp(total) * h_state + mm(us, stack(b * e_rest), _TN) + khv
    return out, h_new


def _run_lockstep(chains, delays):
    results = [None] * len(chains)
    live = list(range(len(chains)))
    rnd = 0
    while live:
        still = []
        for i in live:
            if rnd < delays[i]:
                still.append(i)
                continue
            try:
                next(chains[i])
                still.append(i)
            except StopIteration as stop:
                results[i] = stop.value
        live = still
        rnd += 1
    return results


def _wkv_kernel(rf_ref, vf_ref, kkf_ref, lwf_ref, kdf_ref, asf_ref,
                rb_ref, vb_ref, kkb_ref, lwb_ref, kdb_ref, asb_ref,
                of_ref, ob_ref, hf_ref, hb_ref):
    @pl.when(pl.program_id(2) == 0)
    def _():
        hf_ref[...] = jnp.zeros_like(hf_ref)
        hb_ref[...] = jnp.zeros_like(hb_ref)

    chains = []
    for g in range(hf_ref.shape[0]):
        ls = slice(g * LANES, (g + 1) * LANES)
        fwd = [ref[0, :, ls].astype(F32) for ref in (rf_ref, lwf_ref, kdf_ref, vf_ref, kkf_ref, asf_ref)]
        bwd = [ref[0, :, ls].astype(F32) for ref in (rb_ref, lwb_ref, kdb_ref, vb_ref, kkb_ref, asb_ref)]
        chains.append(_wkv_chunk(*fwd, hf_ref[g], reverse=False))
        chains.append(_wkv_chunk(*bwd, hb_ref[g], reverse=True))
    results = _run_lockstep(chains, [WKV_STAGGER * (i % 2) for i in range(len(chains))])
    for g in range(hf_ref.shape[0]):
        ls = slice(g * LANES, (g + 1) * LANES)
        (o_f, h_f), (o_b, h_b) = results[2 * g], results[2 * g + 1]
        of_ref[0, :, ls] = o_f.astype(of_ref.dtype)
        hf_ref[g] = h_f
        ob_ref[0, :, ls] = o_b.astype(ob_ref.dtype)
        hb_ref[g] = h_b


def _wkv_scan(r, v, kk, lw_f, kd_f, as_f, lw_b, kd_b, as_b):
    b, t, w = r.shape
    c = WKV_CHUNK
    nc = t // c
    pairs = min(WKV_PAIRS_PER_STEP, w // LANES)
    wb = pairs * LANES
    fwd = pl.BlockSpec((1, c, wb), lambda i, p, s: (i, s, p))
    bwd = pl.BlockSpec((1, c, wb), lambda i, p, s: (i, nc - 1 - s, p))
    out = jax.ShapeDtypeStruct((b, t, w), BF16)
    state = pltpu.VMEM((pairs, LANES, LANES), F32)
    return pl.pallas_call(
        _wkv_kernel,
        out_shape=[out, out],
        grid=(b, w // wb, nc),
        in_specs=[fwd] * 6 + [bwd] * 6,
        out_specs=[fwd, bwd],
        scratch_shapes=[state, state],
        compiler_params=_params("parallel", "parallel", "arbitrary"),
        name="wkv_scan",
    )(r, v, kk, lw_f, kd_f, as_f, r, v, kk, lw_b, kd_b, as_b)


def _mla_prep_kernel(x_ref, g_ref, wz_ref, qn_ref, kvn_ref, wqn_ref, wqr_ref, wqs_ref, wkn_ref, wv_ref,
                     qg_ref, qgr_ref, qgs_ref, kg_ref, kgr_ref, kgs_ref, cos_ref, sin_ref, q_out, k_out, v_out):
    xn = _rms(x_ref[0], g_ref[...]).astype(BF16)
    z = jnp.dot(xn, wz_ref[...], preferred_element_type=F32)
    cq = _rms(z[:, :Q_LORA], qn_ref[...]).astype(BF16)
    ckv = _rms(z[:, Q_LORA:Q_LORA + KV_LORA], kvn_ref[...]).astype(BF16)
    kr = z[:, MLA_COLS - 2 * ROPE_TILE:MLA_COLS - ROPE_TILE]
    krs = z[:, MLA_COLS - ROPE_TILE:]
    q_nope = jnp.dot(cq, wqn_ref[...], preferred_element_type=F32)
    q_rope = jnp.dot(cq, wqr_ref[...], preferred_element_type=F32)
    q_swap = jnp.dot(cq, wqs_ref[...], preferred_element_type=F32)
    k_nope = jnp.dot(ckv, wkn_ref[...], preferred_element_type=F32)
    val = jnp.dot(ckv, wv_ref[...], preferred_element_type=F32)
    cos = cos_ref[...]
    sin = sin_ref[...]
    scale = math.log2(math.e) / math.sqrt(QK_DIM)
    ones_v = jnp.ones((val.shape[0], V_HEAD), BF16)
    kr_ss = jnp.sum(kr * kr, axis=-1, keepdims=True)
    for h in range(MLA_HEADS):
        nope = slice(h * QK_NOPE, (h + 1) * QK_NOPE)
        rope = slice(h * ROPE_TILE, (h + 1) * ROPE_TILE)
        qn, qr, qs = q_nope[:, nope], q_rope[:, rope], q_swap[:, rope]
        ss = jnp.sum(qn * qn, axis=-1, keepdims=True) + jnp.sum(qr * qr, axis=-1, keepdims=True)
        rs = lax.rsqrt(ss / QK_DIM + NORM_EPS)
        q_out[0, h, :, :QK_NOPE] = (qn * rs * qg_ref[:, :QK_NOPE] * scale).astype(BF16)
        rot = qr * rs * qgr_ref[...] * cos + qs * rs * qgs_ref[...] * sin
        q_out[0, h, :, QK_NOPE:] = (rot[:, :QK_ROPE] * scale).astype(BF16)
        kn = k_nope[:, nope]
        ss = jnp.sum(kn * kn, axis=-1, keepdims=True) + kr_ss
        rs = lax.rsqrt(ss / QK_DIM + NORM_EPS)
        k_out[0, h, :, :QK_NOPE] = (kn * rs * kg_ref[:, :QK_NOPE]).astype(BF16)
        rot = kr * rs * kgr_ref[...] * cos + krs * rs * kgs_ref[...] * sin
        k_out[0, h, :, QK_NOPE:] = rot[:, :QK_ROPE].astype(BF16)
        v_out[0, h, :, :V_HEAD] = val[:, nope].astype(BF16)
        v_out[0, h, :, V_HEAD:] = ones_v


def _mla_prep(x, g, wz, q_norm, kv_norm, wqn, wqr, wqs, wkn, wv, qg, qgr, qgs, kg, kgr, kgs, cos2, sin2, *, tm=512):
    b, t, d = x.shape
    h = MLA_HEADS
    return pl.pallas_call(
        _mla_prep_kernel,
        out_shape=[
            jax.ShapeDtypeStruct((b, h, t, QK_DIM), BF16),
            jax.ShapeDtypeStruct((b, h, t, QK_DIM), BF16),
            jax.ShapeDtypeStruct((b, h, t, 2 * V_HEAD), BF16),
        ],
        grid=(b, t // tm),
        in_specs=[
            pl.BlockSpec((1, tm, d), lambda i, j: (i, j, 0)),
            _const_spec((1, d)),
            _const_spec(wz.shape),
            _const_spec((1, Q_LORA)),
            _const_spec((1, KV_LORA)),
            _const_spec(wqn.shape),
            _const_spec(wqr.shape),
            _const_spec(wqs.shape),
            _const_spec(wkn.shape),
            _const_spec(wv.shape),
            _const_spec((1, QK_DIM)),
            _const_spec((1, ROPE_TILE)),
            _const_spec((1, ROPE_TILE)),
            _const_spec((1, QK_DIM)),
            _const_spec((1, ROPE_TILE)),
            _const_spec((1, ROPE_TILE)),
            pl.BlockSpec((tm, ROPE_TILE), lambda i, j: (j, 0)),
            pl.BlockSpec((tm, ROPE_TILE), lambda i, j: (j, 0)),
        ],
        out_specs=[
            pl.BlockSpec((1, h, tm, QK_DIM), lambda i, j: (i, 0, j, 0)),
            pl.BlockSpec((1, h, tm, QK_DIM), lambda i, j: (i, 0, j, 0)),
            pl.BlockSpec((1, h, tm, 2 * V_HEAD), lambda i, j: (i, 0, j, 0)),
        ],
        compiler_params=_params("parallel", "parallel"),
        name="mla_prep",
    )(x, g, wz, q_norm, kv_norm, wqn, wqr, wqs, wkn, wv, qg, qgr, qgs, kg, kgr, kgs, cos2, sin2)


def _attn_kernel(q_ref, k_ref, v_ref, o_ref, s_scr, p_scr, rmax_scr, m_ref, acc_ref, *, sub):
    n_sub = k_ref.shape[2] // sub
    assert n_sub == 1 or n_sub % 2 == 0
    q = q_ref[0, 0]
    dv = o_ref.shape[2]

    def scores(j, slot):
        rows = pl.ds(pl.multiple_of(j * sub, sub), sub)
        s = lax.dot_general(q, k_ref[0, 0, rows, :], _NT, preferred_element_type=F32)
        s_scr[slot] = s
        rmax_scr[slot] = jnp.broadcast_to(jnp.max(s, axis=-1, keepdims=True), rmax_scr.shape[1:])

    def weighted(j, slot):
        rows = pl.ds(pl.multiple_of(j * sub, sub), sub)
        return jnp.dot(p_scr[slot], v_ref[0, 0, rows, :], preferred_element_type=F32)

    def softmax(slot):
        m_old = m_ref[...]
        m_new = jnp.maximum(m_old, rmax_scr[slot])
        alpha = jnp.exp2(m_old - m_new)
        m_ref[...] = m_new
        p_scr[slot] = jnp.exp2(s_scr[slot] - jnp.concatenate([m_new] * (sub // LANES), axis=1)).astype(BF16)
        return jnp.concatenate([alpha] * (acc_ref.shape[1] // LANES), axis=1)

    def stage(j, slot, first=False, last=False):
        if not last:
            scores(j + 1, 1 - slot)
        pv = None if first else weighted(j - 1, 1 - slot)
        alpha = softmax(slot)
        if not first:
            acc_ref[...] = alpha * (acc_ref[...] + pv)

    m_ref[...] = jnp.full_like(m_ref, -jnp.inf)
    acc_ref[...] = jnp.zeros_like(acc_ref)
    scores(0, 0)
    if n_sub == 1:
        stage(0, 0, first=True, last=True)
    else:
        stage(0, 0, first=True)

        for j in range(1, n_sub - 1):
            stage(j, j % 2)
        stage(n_sub - 1, 1, last=True)
    acc = acc_ref[...] + weighted(n_sub - 1, (n_sub - 1) % 2)
    o_ref[0] = (acc[:, :dv] / acc[:, dv:]).astype(o_ref.dtype)


def _attention(q, k, v, *, tq=512, sub=2048):
    b, h, t, dq = q.shape
    dv = v.shape[-1] // 2
    tq = min(tq, t)
    sub = min(sub, t // 2)
    return pl.pallas_call(
        functools.partial(_attn_kernel, sub=sub),
        out_shape=jax.ShapeDtypeStruct((b, t, h * dv), BF16),
        grid=(b, h, t // tq),
        in_specs=[
            pl.BlockSpec((1, 1, tq, dq), lambda i, j, qi: (i, j, qi, 0)),
            pl.BlockSpec((1, 1, t, dq), lambda i, j, qi: (i, j, 0, 0)),
            pl.BlockSpec((1, 1, t, 2 * dv), lambda i, j, qi: (i, j, 0, 0)),
        ],
        out_specs=pl.BlockSpec((1, tq, dv), lambda i, j, qi: (i, qi, j)),
        scratch_shapes=[
            pltpu.VMEM((2, tq, sub), F32),
            pltpu.VMEM((2, tq, sub), BF16),
            pltpu.VMEM((2, tq, LANES), F32),
            pltpu.VMEM((tq, LANES), F32),
            pltpu.VMEM((tq, 2 * dv), F32),
        ],
        compiler_params=_params("parallel", "parallel", "arbitrary"),
        name="mla_attention",
    )(q, k, v)


def _group_norm(x, g, b):
    mean = _head_sum(x) * (1.0 / RWKV_HEAD_DIM)
    xc = x - mean
    var = _head_sum(xc * xc) * (1.0 / RWKV_HEAD_DIM)
    return xc * lax.rsqrt(var + GN_EPS) * g + b


def _merge_kernel(x_ref, wf_ref, wb_ref, bonus_ref, g_ref, ob_ref, ga_ref, gb_ref,
                  lng_ref, lnb_ref, wa_ref, wbr_ref, wo_ref, o_ref):
    lng = lng_ref[...]
    lnb = lnb_ref[...]
    gn = _group_norm(wf_ref[...].astype(F32), lng, lnb) + _group_norm(wb_ref[...].astype(F32), lng, lnb)
    o_a = (gn + bonus_ref[...].astype(F32)) * g_ref[...].astype(F32)
    br_a = jnp.dot(o_a.astype(BF16), wa_ref[...], preferred_element_type=F32)
    br_b = jnp.dot(ob_ref[...], wbr_ref[...], preferred_element_type=F32)
    merged = ga_ref[...].astype(F32) * br_a + gb_ref[...].astype(F32) * br_b
    o_ref[...] = x_ref[...] + jnp.dot(merged.astype(BF16), wo_ref[...], preferred_element_type=F32)


def _merge(x, wkv_f, wkv_b, bonus, g, o_b, gates, ln_g, ln_b, w_a, w_b, w_o, *, tm=256):
    m, d = x.shape
    w = RWKV_WIDTH
    row_w = pl.BlockSpec((tm, w), lambda i: (i, 0))
    return pl.pallas_call(
        _merge_kernel,
        out_shape=jax.ShapeDtypeStruct((m, d), F32),
        grid=(m // tm,),
        in_specs=[
            pl.BlockSpec((tm, d), lambda i: (i, 0)),
            row_w, row_w, row_w, row_w, row_w,
            pl.BlockSpec((tm, d), lambda i: (i, 0)),
            pl.BlockSpec((tm, d), lambda i: (i, 1)),
            _const_spec((1, w)),
            _const_spec((1, w)),
            _const_spec(w_a.shape),
            _const_spec(w_b.shape),
            _const_spec(w_o.shape),
        ],
        out_specs=pl.BlockSpec((tm, d), lambda i: (i, 0)),
        compiler_params=_params("parallel"),
        name="merge",
    )(x, wkv_f, wkv_b, bonus, g, o_b, gates, gates, ln_g, ln_b, w_a, w_b, w_o)


def _rope_tile(x):
    return jnp.pad(x, [(0, 0)] * (x.ndim - 1) + [(0, ROPE_TILE - x.shape[-1])])


def _swap_halves(x, axis=-1):
    lo, hi = jnp.split(x, 2, axis=axis)
    return jnp.concatenate([hi, lo], axis=axis)


def _prepare_weights(p):
    w = {}
    w['ffn1_w_in'] = p['ffn1_w_in'].astype(BF16)
    w['ffn1_w_out'] = p['ffn1_w_out'].astype(BF16)
    w['ffn2_w_in'] = p['ffn2_w_in'].astype(BF16)
    w['ffn2_w_out'] = p['ffn2_w_out'].astype(BF16)
    w_in = p['w_in']
    w['w_rwkv'] = w_in[:, :RWKV_IN].astype(BF16)
    mla = w_in[:, RWKV_IN:RWKV_IN + MLA_IN]
    w['w_mla'] = jnp.concatenate(
        [mla[:, :Q_LORA + KV_LORA], _rope_tile(mla[:, MLA_IN - QK_ROPE:]),
         _rope_tile(_swap_halves(mla[:, MLA_IN - QK_ROPE:]))], axis=1).astype(BF16)
    w['w_gate'] = w_in[:, RWKV_IN + MLA_IN:].astype(BF16)

    def lora_rows(mat, slot):
        z = jnp.zeros((LORA_SLAB, RWKV_WIDTH), F32)
        return lax.dynamic_update_slice(z, mat, (slot * DECAY_LORA, 0)).astype(BF16)

    w['w2f'] = lora_rows(p['rwkv_w2'][0], 0)
    w['w2b'] = lora_rows(p['rwkv_w2'][1], 1)
    w['a2f'] = lora_rows(p['rwkv_a2'][0], 2)
    w['a2b'] = lora_rows(p['rwkv_a2'][1], 3)
    w['g2'] = p['rwkv_g2'].astype(BF16)

    w_uq = p['mla_w_uq'].reshape(Q_LORA, MLA_HEADS, QK_DIM)
    w['wqn'] = w_uq[:, :, :QK_NOPE].reshape(Q_LORA, MLA_HEADS * QK_NOPE).astype(BF16)
    wqr = w_uq[:, :, QK_NOPE:]
    w['wqr'] = _rope_tile(wqr).reshape(Q_LORA, MLA_HEADS * ROPE_TILE).astype(BF16)
    w['wqs'] = _rope_tile(_swap_halves(wqr)).reshape(Q_LORA, MLA_HEADS * ROPE_TILE).astype(BF16)
    w_ukv = p['mla_w_ukv'].reshape(KV_LORA, MLA_HEADS, QK_NOPE + V_HEAD)
    w['wkn'] = w_ukv[:, :, :QK_NOPE].reshape(KV_LORA, MLA_HEADS * QK_NOPE).astype(BF16)
    w['wv'] = w_ukv[:, :, QK_NOPE:].reshape(KV_LORA, MLA_HEADS * V_HEAD).astype(BF16)
    w['qgr'] = _rope_tile(p['mla_q_gain'][QK_NOPE:])[None, :]
    w['qgs'] = _rope_tile(_swap_halves(p['mla_q_gain'][QK_NOPE:]))[None, :]
    w['kgr'] = _rope_tile(p['mla_k_gain'][QK_NOPE:])[None, :]
    w['kgs'] = _rope_tile(_swap_halves(p['mla_k_gain'][QK_NOPE:]))[None, :]
    w['w_a'] = p['w_branch_a'].astype(BF16)
    w['w_b'] = p['w_branch_b'].astype(BF16)
    w['w_o'] = p['w_out'].astype(BF16)
    return w


def _rope_tables(t):
    inv = ROPE_BASE ** (-jnp.arange(0, QK_ROPE, 2, dtype=F32) / QK_ROPE)
    ang = jnp.arange(t, dtype=F32)[:, None] * inv[None, :]
    cos, sin = jnp.cos(ang), jnp.sin(ang)
    return _rope_tile(jnp.concatenate([cos, cos], axis=1)), _rope_tile(jnp.concatenate([-sin, sin], axis=1))


def _row(vec):
    return vec.reshape(1, -1)


def _encoder_layer(x, p, w):
    b, t, d = x.shape
    m = b * t
    x0 = x.reshape(m, d)
    x1 = _ffn(x0, _row(p['ffn1_norm']), w['ffn1_w_in'], w['ffn1_w_out'])
    mix_g = _row(p['mix_norm'])
    z_rwkv = _norm_matmul(x1, mix_g, w['w_rwkv'], tm=512, tn=RWKV_IN, name="in_proj_rwkv")
    gates = _norm_matmul(x1, mix_g, w['w_gate'], tm=1024, tn=2048, name="in_proj_gate", gate=True)

    (r, v, kk, g, bonus, lw_f, kd_f, as_f, lw_b, kd_b, as_b) = _rwkv_prep(
        z_rwkv.reshape(b, t, RWKV_IN), _row(p['rwkv_mu']), p['rwkv_w0'], p['rwkv_a0'],
        w['w2f'], w['w2b'], w['a2f'], w['a2b'], w['g2'],
        _row(p['rwkv_k_k']), _row(p['rwkv_k_a']), _row(p['rwkv_r_k']))
    wkv_f, wkv_b = _wkv_scan(r, v, kk, lw_f, kd_f, as_f, lw_b, kd_b, as_b)

    cos2, sin2 = _rope_tables(t)
    q, k, val = _mla_prep(
        x1.reshape(b, t, d), mix_g, w['w_mla'], _row(p['mla_q_norm']), _row(p['mla_kv_norm']),
        w['wqn'], w['wqr'], w['wqs'], w['wkn'], w['wv'],
        _row(p['mla_q_gain']), w['qgr'], w['qgs'], _row(p['mla_k_gain']), w['kgr'], w['kgs'], cos2, sin2)
    o_b = _attention(q, k, val)

    flat = lambda u: u.reshape(m, -1)
    x2 = _merge(x1, flat(wkv_f), flat(wkv_b), flat(bonus), flat(g), flat(o_b), gates,
                _row(p['rwkv_ln_g']), _row(p['rwkv_ln_b']), w['w_a'], w['w_b'], w['w_o'])
    y = _ffn(x2, _row(p['ffn2_norm']), w['ffn2_w_in'], w['ffn2_w_out'], _row(p['out_norm']))
    return y.reshape(b, t, d)


def kernel(x_prompt, x_sample, ffn1_norm, ffn1_w_in, ffn1_w_out, mix_norm, w_in, rwkv_mu, rwkv_w0, rwkv_w2, rwkv_a0, rwkv_a2, rwkv_g2, rwkv_k_k, rwkv_k_a, rwkv_r_k, rwkv_ln_g, rwkv_ln_b, mla_q_norm, mla_w_uq, mla_kv_norm, mla_w_ukv, mla_q_gain, mla_k_gain, w_branch_a, w_branch_b, w_out, ffn2_norm, ffn2_w_in, ffn2_w_out, out_norm):
    stacked = dict(
        ffn1_norm=ffn1_norm, ffn1_w_in=ffn1_w_in, ffn1_w_out=ffn1_w_out, mix_norm=mix_norm, w_in=w_in,
        rwkv_mu=rwkv_mu, rwkv_w0=rwkv_w0, rwkv_w2=rwkv_w2, rwkv_a0=rwkv_a0, rwkv_a2=rwkv_a2,
        rwkv_g2=rwkv_g2, rwkv_k_k=rwkv_k_k, rwkv_k_a=rwkv_k_a, rwkv_r_k=rwkv_r_k,
        rwkv_ln_g=rwkv_ln_g, rwkv_ln_b=rwkv_ln_b, mla_q_norm=mla_q_norm, mla_w_uq=mla_w_uq,
        mla_kv_norm=mla_kv_norm, mla_w_ukv=mla_w_ukv, mla_q_gain=mla_q_gain, mla_k_gain=mla_k_gain,
        w_branch_a=w_branch_a, w_branch_b=w_branch_b, w_out=w_out, ffn2_norm=ffn2_norm,
        ffn2_w_in=ffn2_w_in, ffn2_w_out=ffn2_w_out, out_norm=out_norm)
    y_prompt, y_sample = x_prompt, x_sample
    for layer in range(ffn1_norm.shape[0]):
        p = {name: arr[layer] for name, arr in stacked.items()}
        p['rwkv_r_k'] = p['rwkv_r_k'].reshape(-1)
        w = _prepare_weights(p)
        y_prompt = _encoder_layer(y_prompt, p, w)
        y_sample = _encoder_layer(y_sample, p, w)
    return (y_prompt, y_sample)
```

```python
import functools
import math

import jax
import jax.numpy as jnp
from jax import lax
from jax.experimental import pallas as pl
from jax.experimental.pallas import tpu as pltpu

F32 = jnp.float32
BF16 = jnp.bfloat16

NORM_EPS = 1e-6

LANES = 128
SUBLANES = 8
MXU_DIM = 256
HALO_ROWS = 2 * SUBLANES
VMEM_LIMIT_BYTES = 56 * 1024 * 1024

RWKV_HEADS = 16
RWKV_HEAD_DIM = 64
RWKV_WIDTH = RWKV_HEADS * RWKV_HEAD_DIM
DECAY_LORA = 96
AAA_LORA = 96
GATE_LORA = 256
LORA_SLAB = 2 * DECAY_LORA + 2 * AAA_LORA
GN_EPS = 64e-5
RWKV_IN = 3 * RWKV_WIDTH + LORA_SLAB + GATE_LORA

MLA_HEADS = 8
Q_LORA = 512
KV_LORA = 512
QK_NOPE = 128
QK_ROPE = 64
QK_DIM = QK_NOPE + QK_ROPE
V_HEAD = 128
MLA_IN = Q_LORA + KV_LORA + QK_ROPE
ROPE_TILE = LANES
MLA_COLS = Q_LORA + KV_LORA + 2 * ROPE_TILE
ROPE_BASE = 10000.0

WKV_CHUNK = 64
WKV_DIAG = 16
WKV_PARTS = 1
WKV_PAIRS_PER_STEP = 8
WKV_STAGGER = 2


def _params(*semantics):
    return pltpu.CompilerParams(dimension_semantics=semantics, vmem_limit_bytes=VMEM_LIMIT_BYTES)


def _const_spec(shape):
    zeros = (0,) * len(shape)
    return pl.BlockSpec(shape, lambda *_: zeros, pipeline_mode=pl.Buffered(1))


def _rms(x, g):
    ms = jnp.mean(x * x, axis=-1, keepdims=True)
    return x * lax.rsqrt(ms + NORM_EPS) * g


def _sigmoid(x):
    return 0.5 * jnp.tanh(0.5 * x) + 0.5


def _bf16_parts(x, n):
    if x.dtype == BF16:
        return [x]
    parts = []
    rem = x
    for i in range(n):
        p = rem.astype(BF16)
        parts.append(p)
        if i + 1 < n:
            rem = rem - p.astype(F32)
    return parts


_NN = (((1,), (0,)), ((), ()))
_NT = (((1,), (1,)), ((), ()))
_TN = (((0,), (0,)), ((), ()))


def _dotp(a, b, dims=_NN, pa=1, pb=1):
    a_parts = _bf16_parts(a, pa)
    b_parts = _bf16_parts(b, pb)
    order = max(len(a_parts), len(b_parts)) - 1
    acc = None
    for i, ap in enumerate(a_parts):
        for j, bp in enumerate(b_parts):
            if i + j > order:
                continue
            t = lax.dot_general(ap, bp, dims, preferred_element_type=F32)
            acc = t if acc is None else acc + t
    return acc


def _ffn_kernel(x_ref, g_ref, wg_ref, wu_ref, wo_ref, gn_ref, *rest, final_norm):
    if final_norm:
        o_ref, xn_ref = rest
    else:
        o_ref, yn_ref, xn_ref = rest
    j = pl.program_id(1)

    @pl.when(j == 0)
    def _():
        xn_ref[...] = _rms(x_ref[...], g_ref[...]).astype(BF16)
        o_ref[...] = jnp.zeros_like(o_ref)

    xn = xn_ref[...]
    gate = jnp.dot(xn, wg_ref[...], preferred_element_type=F32)
    up = jnp.dot(xn, wu_ref[...], preferred_element_type=F32)
    h = (gate * _sigmoid(gate) * up).astype(BF16)
    o_ref[...] += jnp.dot(h, wo_ref[...], preferred_element_type=F32)

    @pl.when(j == pl.num_programs(1) - 1)
    def _():
        y = x_ref[...] + 0.5 * o_ref[...]
        yn = _rms(y, gn_ref[...])
        if final_norm:
            o_ref[...] = yn
        else:
            o_ref[...] = y
            yn_ref[...] = yn.astype(BF16)


def _ffn(x, g, w_in, w_out, g_next, *, final_norm, tm=512, tf=512):
    m, d = x.shape
    n_ff = w_out.shape[0]
    nj = n_ff // tf
    row = pl.BlockSpec((tm, d), lambda i, j: (i, 0))
    vec = pl.BlockSpec((1, d), lambda i, j: (0, 0))
    wide = jax.ShapeDtypeStruct((m, d), F32)
    return pl.pallas_call(
        functools.partial(_ffn_kernel, final_norm=final_norm),
        out_shape=wide if final_norm else [wide, jax.ShapeDtypeStruct((m, d), BF16)],
        grid=(m // tm, nj),
        in_specs=[
            row,
            vec,
            pl.BlockSpec((d, tf), lambda i, j: (0, j)),
            pl.BlockSpec((d, tf), lambda i, j: (0, j + nj)),
            pl.BlockSpec((tf, d), lambda i, j: (j, 0)),
            vec,
        ],
        out_specs=row if final_norm else [row, row],
        scratch_shapes=[pltpu.VMEM((tm, d), BF16)],
        compiler_params=_params("parallel", "arbitrary"),
        name="ffn",
    )(x, g, w_in, w_in, w_out, g_next)


def _proj_kernel(xn_ref, w_ref, o_ref, *, gate):
    z = jnp.dot(xn_ref[...], w_ref[...], preferred_element_type=F32)
    o_ref[...] = (_sigmoid(z) if gate else z).astype(o_ref.dtype)


def _proj(xn, w, *, tm, tn, name, gate=False):
    m, d = xn.shape
    n = w.shape[1]
    w_spec = _const_spec((d, n)) if tn == n else pl.BlockSpec((d, tn), lambda i, j: (0, j))
    return pl.pallas_call(
        functools.partial(_proj_kernel, gate=gate),
        out_shape=jax.ShapeDtypeStruct((m, n), BF16),
        grid=(m // tm, n // tn),
        in_specs=[pl.BlockSpec((tm, d), lambda i, j: (i, 0)), w_spec],
        out_specs=pl.BlockSpec((tm, tn), lambda i, j: (i, j)),
        compiler_params=_params("parallel", "arbitrary"),
        name=name,
    )(xn, w)


def _head_sum(x):
    wide = MXU_DIM
    r = lax.broadcasted_iota(jnp.int32, (wide, wide), 0) // RWKV_HEAD_DIM
    c = lax.broadcasted_iota(jnp.int32, (wide, wide), 1) // RWKV_HEAD_DIM
    ones = (r == c).astype(BF16)
    tiles = [
        _dotp(x[:, j * wide:(j + 1) * wide], ones, pa=1, pb=1)
        for j in range(x.shape[1] // wide)
    ]
    return jnp.concatenate(tiles, axis=1)


def _rwkv_prep_kernel(z_ref, zp_ref, zn_ref, mu_ref, w0_ref, a0_ref, w2f_ref, w2b_ref, a2f_ref,
                      a2b_ref, g2_ref, kk_ref, ka_ref, rk_ref,
                      r_out, v_out, kkn_out, g_out, bonus_out,
                      lwf_out, kdf_out, asf_out, lwb_out, kdb_out, asb_out):
    j = pl.program_id(1)
    nj = pl.num_programs(1)
    z = z_ref[0].astype(F32)
    tm = z.shape[0]
    row = lax.broadcasted_iota(jnp.int32, z.shape, 0)
    halo_prev = jnp.where(j > 0, zp_ref[0, HALO_ROWS - 1:HALO_ROWS, :].astype(F32), 0.0)
    halo_next = jnp.where(j < nj - 1, zn_ref[0, 0:1, :].astype(F32), 0.0)
    prev = jnp.where(row == 0, halo_prev, pltpu.roll(z, 1, axis=0))
    nxt = jnp.where(row == tm - 1, halo_next, pltpu.roll(z, tm - 1, axis=0))
    zs = z + mu_ref[...] * (0.5 * (prev + nxt) - z)

    w = RWKV_WIDTH
    r = zs[:, 0:w]
    k = zs[:, w:2 * w]
    v = zs[:, 2 * w:3 * w]
    lora = zs[:, 3 * w:3 * w + LORA_SLAB]
    gd = zs[:, 3 * w + LORA_SLAB:]

    g_out[0] = jnp.dot(_sigmoid(gd).astype(BF16), g2_ref[...], preferred_element_type=F32).astype(g_out.dtype)
    kk = k * kk_ref[...]
    kk = kk * lax.rsqrt(_head_sum(kk * kk) + 1e-12)
    r_out[0] = r.astype(r_out.dtype)
    v_out[0] = v.astype(v_out.dtype)
    kkn_out[0] = kk.astype(kkn_out.dtype)

    lora_t = jnp.tanh(lora).astype(BF16)
    lora_l = lora.astype(BF16)
    bonus = None
    for d, (w2_ref, a2_ref, lw_out, kd_out, as_out) in enumerate(
            ((w2f_ref, a2f_ref, lwf_out, kdf_out, asf_out), (w2b_ref, a2b_ref, lwb_out, kdb_out, asb_out))):
        w_pre = w0_ref[d:d + 1, :] + jnp.dot(lora_t, w2_ref[...], preferred_element_type=F32)
        lw_out[0] = -math.exp(-0.5) * _sigmoid(w_pre)
        a = _sigmoid(a0_ref[d:d + 1, :] + jnp.dot(lora_l, a2_ref[...], preferred_element_type=F32))
        k_d = k * (1.0 + (a - 1.0) * ka_ref[...])
        kd_out[0] = k_d.astype(kd_out.dtype)
        as_out[0] = a.astype(as_out.dtype)
        b_d = _head_sum(r * k_d * rk_ref[...]) * v
        bonus = b_d if bonus is None else bonus + b_d
    bonus_out[0] = bonus.astype(bonus_out.dtype)


def _rwkv_prep(z, mu, w0, a0, w2f, w2b, a2f, a2b, g2, k_k, k_a, r_k, *, tm=256):
    b, t, zc = z.shape
    w = RWKV_WIDTH
    nb = tm // HALO_ROWS
    last_halo = t // HALO_ROWS - 1
    narrow = jax.ShapeDtypeStruct((b, t, w), BF16)
    wide = jax.ShapeDtypeStruct((b, t, w), F32)
    row_spec = pl.BlockSpec((1, tm, w), lambda i, j: (i, j, 0))
    return pl.pallas_call(
        _rwkv_prep_kernel,
        out_shape=[narrow] * 5 + [wide, narrow, narrow] * 2,
        grid=(b, t // tm),
        in_specs=[
            pl.BlockSpec((1, tm, zc), lambda i, j: (i, j, 0)),
            pl.BlockSpec((1, HALO_ROWS, zc), lambda i, j: (i, jnp.maximum(j * nb - 1, 0), 0)),
            pl.BlockSpec((1, HALO_ROWS, zc), lambda i, j: (i, jnp.minimum((j + 1) * nb, last_halo), 0)),
            _const_spec((1, zc)),
            _const_spec((2, w)),
            _const_spec((2, w)),
            _const_spec((LORA_SLAB, w)),
            _const_spec((LORA_SLAB, w)),
            _const_spec((LORA_SLAB, w)),
            _const_spec((LORA_SLAB, w)),
            _const_spec((GATE_LORA, w)),
            _const_spec((1, w)),
            _const_spec((1, w)),
            _const_spec((1, w)),
        ],
        out_specs=[row_spec] * 11,
        compiler_params=_params("parallel", "parallel"),
        name="rwkv_prep",
    )(z, z, z, mu, w0, a0, w2f, w2b, a2f, a2b, g2, k_k, k_a, r_k)


def _wkv_chunk(r, lw, k, v, kk, asig, h_state, reverse):
    c = r.shape[0]
    n = 2 * c
    mm = functools.partial(_dotp, pa=WKV_PARTS, pb=WKV_PARTS)
    a = -kk
    b = kk * asig

    ti = lax.broadcasted_iota(jnp.int32, (c, c), 0)
    si = lax.broadcasted_iota(jnp.int32, (c, c), 1)
    tri = ((si >= ti) if reverse else (si <= ti)).astype(BF16)
    cum = _dotp(tri, lw, pa=1, pb=2)
    yield

    cum_x = cum - lw
    total = jnp.sum(lw, axis=0, keepdims=True)
    e_in = jnp.exp(cum)
    e_ex = jnp.exp(cum_x)
    e_neg = jnp.exp(-cum)
    e_rest = jnp.exp(total - cum)

    lane = lax.broadcasted_iota(jnp.int32, (c, LANES), 1)
    first = lane < RWKV_HEAD_DIM

    def stack(x):
        return jnp.concatenate([jnp.where(first, x, 0.0), jnp.where(first, 0.0, x)], axis=0)

    lhs = jnp.concatenate([stack(a * e_ex), stack(r * e_in)], axis=0)
    rhs = jnp.concatenate([stack(b * e_neg), stack(k * e_neg)], axis=0)
    both = mm(lhs, jnp.concatenate([rhs, h_state], axis=0), _NT)
    aa = both[:, :2 * n]
    x1 = both[:, 2 * n:]
    vs = stack(v)
    khv = mm(vs, stack(k * e_rest), _TN)
    yield

    row = lax.broadcasted_iota(jnp.int32, (n, n), 0)
    col = lax.broadcasted_iota(jnp.int32, (n, n), 1)
    t_in = row % c
    s_in = col % c
    strict = (s_in > t_in) if reverse else (s_in < t_in)
    incl = (s_in >= t_in) if reverse else (s_in <= t_in)
    n_ab = jnp.where(strict, aa[:n, :n], 0.0)
    a_ak = jnp.where(strict, aa[:n, n:], 0.0)
    a_rb = jnp.where(incl, aa[n:, :n], 0.0)
    a_rk = jnp.where(incl, aa[n:, n:], 0.0)

    eye = (row == col).astype(F32)
    diag = (row // WKV_DIAG) == (col // WKV_DIAG)
    n_d = jnp.where(diag, n_ab, 0.0)
    n_o = jnp.where(diag, 0.0, n_ab)
    x = eye + n_d
    pw = mm(n_d, n_d)
    av = mm(jnp.concatenate([a_ak, a_rk], axis=0), vs)
    yield
    for _ in range(int(math.log2(WKV_DIAG)) - 2):
        stacked = mm(jnp.concatenate([x, pw], axis=0), pw)
        x = x + stacked[:n]
        pw = stacked[n:]
        yield
    t_d = x + mm(x, pw)
    yield
    e1 = mm(t_d, n_o)
    yield
    y = eye + e1
    pw = e1
    for _ in range(int(math.log2(c // WKV_DIAG)) - 1):
        pw = mm(pw, pw)
        yield
        y = y + mm(y, pw)
        yield
    t_inv = mm(y, t_d)
    yield
    us = mm(t_inv, x1[:n] + av[:n])
    yield
    os_ = x1[n:] + mm(a_rb, us) + av[n:]
    out = os_[:c] + os_[c:]
    h_new = jnp.exp(total) * h_state + mm(us, stack(b * e_rest), _TN) + khv
    return out, h_new


def _run_lockstep(chains, delays):
    results = [None] * len(chains)
    live = list(range(len(chains)))
    rnd = 0
    while live:
        still = []
        for i in live:
            if rnd < delays[i]:
                still.append(i)
                continue
            try:
                next(chains[i])
                still.append(i)
            except StopIteration as stop:
                results[i] = stop.value
        live = still
        rnd += 1
    return results


def _wkv_kernel(rf_ref, vf_ref, kkf_ref, lwf_ref, kdf_ref, asf_ref,
                rb_ref, vb_ref, kkb_ref, lwb_ref, kdb_ref, asb_ref,
                of_ref, ob_ref, hf_ref, hb_ref):
    @pl.when(pl.program_id(2) == 0)
    def _():
        hf_ref[...] = jnp.zeros_like(hf_ref)
        hb_ref[...] = jnp.zeros_like(hb_ref)

    chains = []
    for g in range(hf_ref.shape[0]):
        ls = slice(g * LANES, (g + 1) * LANES)
        fwd = [ref[0, :, ls].astype(F32) for ref in (rf_ref, lwf_ref, kdf_ref, vf_ref, kkf_ref, asf_ref)]
        bwd = [ref[0, :, ls].astype(F32) for ref in (rb_ref, lwb_ref, kdb_ref, vb_ref, kkb_ref, asb_ref)]
        chains.append(_wkv_chunk(*fwd, hf_ref[g], reverse=False))
        chains.append(_wkv_chunk(*bwd, hb_ref[g], reverse=True))
    results = _run_lockstep(chains, [WKV_STAGGER * (i % 2) for i in range(len(chains))])
    for g in range(hf_ref.shape[0]):
        ls = slice(g * LANES, (g + 1) * LANES)
        (o_f, h_f), (o_b, h_b) = results[2 * g], results[2 * g + 1]
        of_ref[0, :, ls] = o_f.astype(of_ref.dtype)
        hf_ref[g] = h_f
        ob_ref[0, :, ls] = o_b.astype(ob_ref.dtype)
        hb_ref[g] = h_b


def _wkv_scan(r, v, kk, lw_f, kd_f, as_f, lw_b, kd_b, as_b):
    b, t, w = r.shape
    c = WKV_CHUNK
    nc = t // c
    pairs = min(WKV_PAIRS_PER_STEP, w // LANES)
    wb = pairs * LANES
    fwd = pl.BlockSpec((1, c, wb), lambda i, p, s: (i, s, p))
    bwd = pl.BlockSpec((1, c, wb), lambda i, p, s: (i, nc - 1 - s, p))
    out = jax.ShapeDtypeStruct((b, t, w), BF16)
    state = pltpu.VMEM((pairs, LANES, LANES), F32)
    return pl.pallas_call(
        _wkv_kernel,
        out_shape=[out, out],
        grid=(b, w // wb, nc),
        in_specs=[fwd] * 6 + [bwd] * 6,
        out_specs=[fwd, bwd],
        scratch_shapes=[state, state],
        compiler_params=_params("parallel", "parallel", "arbitrary"),
        name="wkv_scan",
    )(r, v, kk, lw_f, kd_f, as_f, r, v, kk, lw_b, kd_b, as_b)


def _mla_prep_kernel(xn_ref, wz_ref, qn_ref, kvn_ref, wqn_ref, wqr_ref, wqs_ref, wkn_ref, wv_ref,
                     qg_ref, qgr_ref, qgs_ref, kg_ref, kgr_ref, kgs_ref, cos_ref, sin_ref, q_out, k_out, v_out):
    z = jnp.dot(xn_ref[0], wz_ref[...], preferred_element_type=F32)
    cq = _rms(z[:, :Q_LORA], qn_ref[...]).astype(BF16)
    ckv = _rms(z[:, Q_LORA:Q_LORA + KV_LORA], kvn_ref[...]).astype(BF16)
    kr = z[:, MLA_COLS - 2 * ROPE_TILE:MLA_COLS - ROPE_TILE]
    krs = z[:, MLA_COLS - ROPE_TILE:]
    q_nope = jnp.dot(cq, wqn_ref[...], preferred_element_type=F32)
    q_rope = jnp.dot(cq, wqr_ref[...], preferred_element_type=F32)
    q_swap = jnp.dot(cq, wqs_ref[...], preferred_element_type=F32)
    k_nope = jnp.dot(ckv, wkn_ref[...], preferred_element_type=F32)
    val = jnp.dot(ckv, wv_ref[...], preferred_element_type=F32)
    cos = cos_ref[...]
    sin = sin_ref[...]
    scale = math.log2(math.e) / math.sqrt(QK_DIM)
    ones_v = jnp.ones((val.shape[0], V_HEAD), BF16)
    kr_ss = jnp.sum(kr * kr, axis=-1, keepdims=True)
    for h in range(MLA_HEADS):
        nope = slice(h * QK_NOPE, (h + 1) * QK_NOPE)
        rope = slice(h * ROPE_TILE, (h + 1) * ROPE_TILE)
        qn, qr, qs = q_nope[:, nope], q_rope[:, rope], q_swap[:, rope]
        ss = jnp.sum(qn * qn, axis=-1, keepdims=True) + jnp.sum(qr * qr, axis=-1, keepdims=True)
        rs = lax.rsqrt(ss / QK_DIM + NORM_EPS)
        q_out[0, h, :, :QK_NOPE] = (qn * rs * qg_ref[:, :QK_NOPE] * scale).astype(BF16)
        rot = qr * rs * qgr_ref[...] * cos + qs * rs * qgs_ref[...] * sin
        q_out[0, h, :, QK_NOPE:] = (rot[:, :QK_ROPE] * scale).astype(BF16)
        kn = k_nope[:, nope]
        ss = jnp.sum(kn * kn, axis=-1, keepdims=True) + kr_ss
        rs = lax.rsqrt(ss / QK_DIM + NORM_EPS)
        k_out[0, h, :, :QK_NOPE] = (kn * rs * kg_ref[:, :QK_NOPE]).astype(BF16)
        rot = kr * rs * kgr_ref[...] * cos + krs * rs * kgs_ref[...] * sin
        k_out[0, h, :, QK_NOPE:] = rot[:, :QK_ROPE].astype(BF16)
        v_out[0, h, :, :V_HEAD] = val[:, nope].astype(BF16)
        v_out[0, h, :, V_HEAD:] = ones_v


def _mla_prep(xn, wz, q_norm, kv_norm, wqn, wqr, wqs, wkn, wv, qg, qgr, qgs, kg, kgr, kgs, cos2, sin2, *, tm=512):
    b, t, d = xn.shape
    h = MLA_HEADS
    return pl.pallas_call(
        _mla_prep_kernel,
        out_shape=[
            jax.ShapeDtypeStruct((b, h, t, QK_DIM), BF16),
            jax.ShapeDtypeStruct((b, h, t, QK_DIM), BF16),
            jax.ShapeDtypeStruct((b, h, t, 2 * V_HEAD), BF16),
        ],
        grid=(b, t // tm),
        in_specs=[
            pl.BlockSpec((1, tm, d), lambda i, j: (i, j, 0)),
            _const_spec(wz.shape),
            _const_spec((1, Q_LORA)),
            _const_spec((1, KV_LORA)),
            _const_spec(wqn.shape),
            _const_spec(wqr.shape),
            _const_spec(wqs.shape),
            _const_spec(wkn.shape),
            _const_spec(wv.shape),
            _const_spec((1, QK_DIM)),
            _const_spec((1, ROPE_TILE)),
            _const_spec((1, ROPE_TILE)),
            _const_spec((1, QK_DIM)),
            _const_spec((1, ROPE_TILE)),
            _const_spec((1, ROPE_TILE)),
            pl.BlockSpec((tm, ROPE_TILE), lambda i, j: (j, 0)),
            pl.BlockSpec((tm, ROPE_TILE), lambda i, j: (j, 0)),
        ],
        out_specs=[
            pl.BlockSpec((1, h, tm, QK_DIM), lambda i, j: (i, 0, j, 0)),
            pl.BlockSpec((1, h, tm, QK_DIM), lambda i, j: (i, 0, j, 0)),
            pl.BlockSpec((1, h, tm, 2 * V_HEAD), lambda i, j: (i, 0, j, 0)),
        ],
        compiler_params=_params("parallel", "parallel"),
        name="mla_prep",
    )(xn, wz, q_norm, kv_norm, wqn, wqr, wqs, wkn, wv, qg, qgr, qgs, kg, kgr, kgs, cos2, sin2)


def _attn_kernel(q_ref, k_ref, v_ref, o_ref, s_scr, p_scr, rmax_scr, m_ref, acc_ref, *, sub):
    n_sub = k_ref.shape[2] // sub
    assert n_sub == 1 or n_sub % 2 == 0
    q = q_ref[0, 0]
    dv = o_ref.shape[2]

    def scores(j, slot):
        rows = pl.ds(pl.multiple_of(j * sub, sub), sub)
        s = lax.dot_general(q, k_ref[0, 0, rows, :], _NT, preferred_element_type=F32)
        s_scr[slot] = s
        rmax_scr[slot] = jnp.broadcast_to(jnp.max(s, axis=-1, keepdims=True), rmax_scr.shape[1:])

    def weighted(j, slot):
        rows = pl.ds(pl.multiple_of(j * sub, sub), sub)
        return jnp.dot(p_scr[slot], v_ref[0, 0, rows, :], preferred_element_type=F32)

    def softmax(slot):
        m_old = m_ref[...]
        m_new = jnp.maximum(m_old, rmax_scr[slot])
        alpha = jnp.exp2(m_old - m_new)
        m_ref[...] = m_new
        p_scr[slot] = jnp.exp2(s_scr[slot] - jnp.concatenate([m_new] * (sub // LANES), axis=1)).astype(BF16)
        return jnp.concatenate([alpha] * (acc_ref.shape[1] // LANES), axis=1)

    def stage(j, slot, first=False, last=False):
        if not last:
            scores(j + 1, 1 - slot)
        pv = None if first else weighted(j - 1, 1 - slot)
        alpha = softmax(slot)
        if not first:
            acc_ref[...] = alpha * (acc_ref[...] + pv)

    m_ref[...] = jnp.full_like(m_ref, -jnp.inf)
    acc_ref[...] = jnp.zeros_like(acc_ref)
    scores(0, 0)
    if n_sub == 1:
        stage(0, 0, first=True, last=True)
    else:
        stage(0, 0, first=True)

        for j in range(1, n_sub - 1):
            stage(j, j % 2)
        stage(n_sub - 1, 1, last=True)
    acc = acc_ref[...] + weighted(n_sub - 1, (n_sub - 1) % 2)
    o_ref[0] = (acc[:, :dv] / acc[:, dv:]).astype(o_ref.dtype)


def _attention(q, k, v, *, tq=512, sub=2048):
    b, h, t, dq = q.shape
    dv = v.shape[-1] // 2
    tq = min(tq, t)
    sub = min(sub, t // 2)
    return pl.pallas_call(
        functools.partial(_attn_kernel, sub=sub),
        out_shape=jax.ShapeDtypeStruct((b, t, h * dv), BF16),
        grid=(b, h, t // tq),
        in_specs=[
            pl.BlockSpec((1, 1, tq, dq), lambda i, j, qi: (i, j, qi, 0)),
            pl.BlockSpec((1, 1, t, dq), lambda i, j, qi: (i, j, 0, 0)),
            pl.BlockSpec((1, 1, t, 2 * dv), lambda i, j, qi: (i, j, 0, 0)),
        ],
        out_specs=pl.BlockSpec((1, tq, dv), lambda i, j, qi: (i, qi, j)),
        scratch_shapes=[
            pltpu.VMEM((2, tq, sub), F32),
            pltpu.VMEM((2, tq, sub), BF16),
            pltpu.VMEM((2, tq, LANES), F32),
            pltpu.VMEM((tq, LANES), F32),
            pltpu.VMEM((tq, 2 * dv), F32),
        ],
        compiler_params=_params("parallel", "parallel", "arbitrary"),
        name="mla_attention",
    )(q, k, v)


def _group_norm(x, g, b):
    mean = _head_sum(x) * (1.0 / RWKV_HEAD_DIM)
    xc = x - mean
    var = _head_sum(xc * xc) * (1.0 / RWKV_HEAD_DIM)
    return xc * lax.rsqrt(var + GN_EPS) * g + b


def _merge_kernel(x_ref, wf_ref, wb_ref, bonus_ref, g_ref, ob_ref, ga_ref, gb_ref,
                  lng_ref, lnb_ref, wa_ref, wbr_ref, wo_ref, o_ref):
    lng = lng_ref[...]
    lnb = lnb_ref[...]
    gn = _group_norm(wf_ref[...].astype(F32), lng, lnb) + _group_norm(wb_ref[...].astype(F32), lng, lnb)
    o_a = (gn + bonus_ref[...].astype(F32)) * g_ref[...].astype(F32)
    br_a = jnp.dot(o_a.astype(BF16), wa_ref[...], preferred_element_type=F32)
    br_b = jnp.dot(ob_ref[...], wbr_ref[...], preferred_element_type=F32)
    merged = ga_ref[...].astype(F32) * br_a + gb_ref[...].astype(F32) * br_b
    o_ref[...] = x_ref[...] + jnp.dot(merged.astype(BF16), wo_ref[...], preferred_element_type=F32)


def _merge(x, wkv_f, wkv_b, bonus, g, o_b, gates, ln_g, ln_b, w_a, w_b, w_o, *, tm=256):
    m, d = x.shape
    w = RWKV_WIDTH
    row_w = pl.BlockSpec((tm, w), lambda i: (i, 0))
    return pl.pallas_call(
        _merge_kernel,
        out_shape=jax.ShapeDtypeStruct((m, d), F32),
        grid=(m // tm,),
        in_specs=[
            pl.BlockSpec((tm, d), lambda i: (i, 0)),
            row_w, row_w, row_w, row_w, row_w,
            pl.BlockSpec((tm, d), lambda i: (i, 0)),
            pl.BlockSpec((tm, d), lambda i: (i, 1)),
            _const_spec((1, w)),
            _const_spec((1, w)),
            _const_spec(w_a.shape),
            _const_spec(w_b.shape),
            _const_spec(w_o.shape),
        ],
        out_specs=pl.BlockSpec((tm, d), lambda i: (i, 0)),
        compiler_params=_params("parallel"),
        name="merge",
    )(x, wkv_f, wkv_b, bonus, g, o_b, gates, gates, ln_g, ln_b, w_a, w_b, w_o)


def _rope_tile(x):
    return jnp.pad(x, [(0, 0)] * (x.ndim - 1) + [(0, ROPE_TILE - x.shape[-1])])


def _swap_halves(x, axis=-1):
    lo, hi = jnp.split(x, 2, axis=axis)
    return jnp.concatenate([hi, lo], axis=axis)


def _prepare_weights(p):
    w = {}
    w['ffn1_w_in'] = p['ffn1_w_in'].astype(BF16)
    w['ffn1_w_out'] = p['ffn1_w_out'].astype(BF16)
    w['ffn2_w_in'] = p['ffn2_w_in'].astype(BF16)
    w['ffn2_w_out'] = p['ffn2_w_out'].astype(BF16)
    w_in = p['w_in']
    w['w_rwkv'] = w_in[:, :RWKV_IN].astype(BF16)
    mla = w_in[:, RWKV_IN:RWKV_IN + MLA_IN]
    w['w_mla'] = jnp.concatenate(
        [mla[:, :Q_LORA + KV_LORA], _rope_tile(mla[:, MLA_IN - QK_ROPE:]),
         _rope_tile(_swap_halves(mla[:, MLA_IN - QK_ROPE:]))], axis=1).astype(BF16)
    w['w_gate'] = w_in[:, RWKV_IN + MLA_IN:].astype(BF16)

    def lora_rows(mat, slot):
        z = jnp.zeros((LORA_SLAB, RWKV_WIDTH), F32)
        return lax.dynamic_update_slice(z, mat, (slot * DECAY_LORA, 0)).astype(BF16)

    w['w2f'] = lora_rows(p['rwkv_w2'][0], 0)
    w['w2b'] = lora_rows(p['rwkv_w2'][1], 1)
    w['a2f'] = lora_rows(p['rwkv_a2'][0], 2)
    w['a2b'] = lora_rows(p['rwkv_a2'][1], 3)
    w['g2'] = p['rwkv_g2'].astype(BF16)

    w_uq = p['mla_w_uq'].reshape(Q_LORA, MLA_HEADS, QK_DIM)
    w['wqn'] = w_uq[:, :, :QK_NOPE].reshape(Q_LORA, MLA_HEADS * QK_NOPE).astype(BF16)
    wqr = w_uq[:, :, QK_NOPE:]
    w['wqr'] = _rope_tile(wqr).reshape(Q_LORA, MLA_HEADS * ROPE_TILE).astype(BF16)
    w['wqs'] = _rope_tile(_swap_halves(wqr)).reshape(Q_LORA, MLA_HEADS * ROPE_TILE).astype(BF16)
    w_ukv = p['mla_w_ukv'].reshape(KV_LORA, MLA_HEADS, QK_NOPE + V_HEAD)
    w['wkn'] = w_ukv[:, :, :QK_NOPE].reshape(KV_LORA, MLA_HEADS * QK_NOPE).astype(BF16)
    w['wv'] = w_ukv[:, :, QK_NOPE:].reshape(KV_LORA, MLA_HEADS * V_HEAD).astype(BF16)
    w['qgr'] = _rope_tile(p['mla_q_gain'][QK_NOPE:])[None, :]
    w['qgs'] = _rope_tile(_swap_halves(p['mla_q_gain'][QK_NOPE:]))[None, :]
    w['kgr'] = _rope_tile(p['mla_k_gain'][QK_NOPE:])[None, :]
    w['kgs'] = _rope_tile(_swap_halves(p['mla_k_gain'][QK_NOPE:]))[None, :]
    w['w_a'] = p['w_branch_a'].astype(BF16)
    w['w_b'] = p['w_branch_b'].astype(BF16)
    w['w_o'] = p['w_out'].astype(BF16)
    return w


def _rope_tables(t):
    inv = ROPE_BASE ** (-jnp.arange(0, QK_ROPE, 2, dtype=F32) / QK_ROPE)
    ang = jnp.arange(t, dtype=F32)[:, None] * inv[None, :]
    cos, sin = jnp.cos(ang), jnp.sin(ang)
    return _rope_tile(jnp.concatenate([cos, cos], axis=1)), _rope_tile(jnp.concatenate([-sin, sin], axis=1))


def _row(vec):
    return vec.reshape(1, -1)


def _encoder_layer(x, p, w):
    b, t, d = x.shape
    m = b * t
    x0 = x.reshape(m, d)
    x1, xn = _ffn(x0, _row(p['ffn1_norm']), w['ffn1_w_in'], w['ffn1_w_out'], _row(p['mix_norm']), final_norm=False)
    z_rwkv = _proj(xn, w['w_rwkv'], tm=512, tn=RWKV_IN, name="in_proj_rwkv")
    gates = _proj(xn, w['w_gate'], tm=1024, tn=2048, name="in_proj_gate", gate=True)

    (r, v, kk, g, bonus, lw_f, kd_f, as_f, lw_b, kd_b, as_b) = _rwkv_prep(
        z_rwkv.reshape(b, t, RWKV_IN), _row(p['rwkv_mu']), p['rwkv_w0'], p['rwkv_a0'],
        w['w2f'], w['w2b'], w['a2f'], w['a2b'], w['g2'],
        _row(p['rwkv_k_k']), _row(p['rwkv_k_a']), _row(p['rwkv_r_k']))
    wkv_f, wkv_b = _wkv_scan(r, v, kk, lw_f, kd_f, as_f, lw_b, kd_b, as_b)

    cos2, sin2 = _rope_tables(t)
    q, k, val = _mla_prep(
        xn.reshape(b, t, d), w['w_mla'], _row(p['mla_q_norm']), _row(p['mla_kv_norm']),
        w['wqn'], w['wqr'], w['wqs'], w['wkn'], w['wv'],
        _row(p['mla_q_gain']), w['qgr'], w['qgs'], _row(p['mla_k_gain']), w['kgr'], w['kgs'], cos2, sin2)
    o_b = _attention(q, k, val)

    flat = lambda u: u.reshape(m, -1)
    x2 = _merge(x1, flat(wkv_f), flat(wkv_b), flat(bonus), flat(g), flat(o_b), gates,
                _row(p['rwkv_ln_g']), _row(p['rwkv_ln_b']), w['w_a'], w['w_b'], w['w_o'])
    y = _ffn(x2, _row(p['ffn2_norm']), w['ffn2_w_in'], w['ffn2_w_out'], _row(p['out_norm']), final_norm=True)
    return y.reshape(b, t, d)


def kernel(x_prompt, x_sample, ffn1_norm, ffn1_w_in, ffn1_w_out, mix_norm, w_in, rwkv_mu, rwkv_w0, rwkv_w2, rwkv_a0, rwkv_a2, rwkv_g2, rwkv_k_k, rwkv_k_a, rwkv_r_k, rwkv_ln_g, rwkv_ln_b, mla_q_norm, mla_w_uq, mla_kv_norm, mla_w_ukv, mla_q_gain, mla_k_gain, w_branch_a, w_branch_b, w_out, ffn2_norm, ffn2_w_in, ffn2_w_out, out_norm):
    stacked = dict(
        ffn1_norm=ffn1_norm, ffn1_w_in=ffn1_w_in, ffn1_w_out=ffn1_w_out, mix_norm=mix_norm, w_in=w_in,
        rwkv_mu=rwkv_mu, rwkv_w0=rwkv_w0, rwkv_w2=rwkv_w2, rwkv_a0=rwkv_a0, rwkv_a2=rwkv_a2,
        rwkv_g2=rwkv_g2, rwkv_k_k=rwkv_k_k, rwkv_k_a=rwkv_k_a, rwkv_r_k=rwkv_r_k,
        rwkv_ln_g=rwkv_ln_g, rwkv_ln_b=rwkv_ln_b, mla_q_norm=mla_q_norm, mla_w_uq=mla_w_uq,
        mla_kv_norm=mla_kv_norm, mla_w_ukv=mla_w_ukv, mla_q_gain=mla_q_gain, mla_k_gain=mla_k_gain,
        w_branch_a=w_branch_a, w_branch_b=w_branch_b, w_out=w_out, ffn2_norm=ffn2_norm,
        ffn2_w_in=ffn2_w_in, ffn2_w_out=ffn2_w_out, out_norm=out_norm)
    y_prompt, y_sample = x_prompt, x_sample
    for layer in range(ffn1_norm.shape[0]):
        p = {name: arr[layer] for name, arr in stacked.items()}
        p['rwkv_r_k'] = p['rwkv_r_k'].reshape(-1)
        w = _prepare_weights(p)
        y_prompt = _encoder_layer(y_prompt, p, w)
        y_sample = _encoder_layer(y_sample, p, w)
    return (y_prompt, y_sample)
```
